```python
import math
import jax
import jax.numpy as jnp
from jax import lax
import numpy as np

D_MODEL = 2048
BATCH = 4
SEQ = 4096
DEPTH = 2

CHUNK = 64
QBLOCK = 128
EPS = 1e-6

POOL_WINDOWS = (2, 4, 8, 16)
N_POOL = 4
POOL_WIDTH = D_MODEL // 2
POOL_GROUP = POOL_WIDTH // N_POOL

DN_HEADS = 8
DN_HEAD_DIM = 128
DN_WIDTH = DN_HEADS * DN_HEAD_DIM
CONV_K = 4

FOX_HEADS = 8
FOX_HEAD_DIM = 128
FOX_WIDTH = FOX_HEADS * FOX_HEAD_DIM

N_BRANCH = 3

IN_SPLITS = (POOL_WIDTH, 3 * DN_WIDTH, DN_HEADS, DN_HEADS, DN_WIDTH, 3 * FOX_WIDTH, FOX_HEADS, N_BRANCH * D_MODEL)
IN_WIDTH = sum(IN_SPLITS)

PEER_HEADS = 8
PEER_KEY_DIM = 256
PEER_HALF = PEER_KEY_DIM // 2
N_SUBKEYS = 128
N_EXPERTS = N_SUBKEYS * N_SUBKEYS
PEER_TOPK = 16
PEER_TOKEN_BLOCK = 128

kernel_name = 'hybrid_pool_deltanet_fox_peer_adaln'


def rms_norm(x, w):
    xf = x.astype(jnp.float32)
    y = xf * lax.rsqrt(jnp.mean(xf * xf, axis=-1, keepdims=True) + EPS)
    return (y * w.astype(jnp.float32)).astype(x.dtype)


def l2_norm(x):
    return x * lax.rsqrt(jnp.sum(x * x, axis=-1, keepdims=True) + EPS)


def split_points():
    pts, acc = [], 0
    for w in IN_SPLITS[:-1]:
        acc += w
        pts.append(acc)
    return pts


def pool_mixer(p, pool_w, pool_scale):
    B, S, _ = p.shape
    pf = p.astype(jnp.float32).reshape(B, S, N_POOL, POOL_GROUP)
    cs = jnp.pad(jnp.cumsum(pf, axis=1), ((0, 0), (1, 0), (0, 0), (0, 0)))
    pos = jnp.arange(S)
    outs = []
    for g, w in enumerate(POOL_WINDOWS):
        csg = cs[:, :, g]
        lagged = jnp.pad(csg[:, :S + 1 - w], ((0, 0), (w - 1, 0), (0, 0)))
        cnt = jnp.minimum(pos + 1, w).astype(jnp.float32)[None, :, None]
        outs.append((csg[:, 1:] - lagged) / cnt - pf[:, :, g])
    pooled = jnp.stack(outs, axis=2).astype(p.dtype)
    y = jnp.einsum('bsgc,gcd->bsgd', pooled, pool_w)
    return y.reshape(B, S, POOL_WIDTH) * pool_scale


def causal_dwconv(x, w):
    return lax.conv_general_dilated(
        x, w[:, None, :].astype(x.dtype), window_strides=(1,), padding=((CONV_K - 1, 0),),
        dimension_numbers=('NWC', 'WIO', 'NWC'), feature_group_count=x.shape[-1])


def gated_delta_rule(q, k, v, g, beta):
    B, H, S, Dk = q.shape
    Dv = v.shape[-1]
    n = S // CHUNK
    q = q * Dk ** -0.5
    kb = k * beta[..., None]
    vb = v * beta[..., None]
    rs = lambda a: a.reshape((B, H, n, CHUNK) + a.shape[3:])
    q, k, kb, vb, g = rs(q), rs(k), rs(kb), rs(vb), rs(g)
    g = jnp.cumsum(g, axis=-1)
    idx = jnp.arange(CHUNK)
    incl = idx[:, None] >= idx[None, :]
    strict = idx[:, None] > idx[None, :]
    decay = jnp.exp(jnp.where(incl, g[..., :, None] - g[..., None, :], -jnp.inf))
    lower = jnp.where(strict, jnp.einsum('bhnid,bhnjd->bhnij', kb, k) * decay, 0.0)
    eye = jnp.eye(CHUNK, dtype=jnp.float32)
    t_mat = lax.linalg.triangular_solve(lower + eye, jnp.broadcast_to(eye, lower.shape),
                                        left_side=True, lower=True, unit_diagonal=True)
    u = jnp.einsum('bhnij,bhnjd->bhnid', t_mat, vb)
    wk = jnp.einsum('bhnij,bhnjd->bhnid', t_mat, kb * jnp.exp(g)[..., None])
    a_intra = jnp.where(incl, jnp.einsum('bhnid,bhnjd->bhnij', q, k) * decay, 0.0)
    q_dec = q * jnp.exp(g)[..., None]
    k_dec = k * jnp.exp(g[..., -1:] - g)[..., None]
    g_last = jnp.exp(g[..., -1])

    def step(state, inp):
        qd_i, kd_i, u_i, wk_i, a_i, gl_i = inp
        v_new = u_i - jnp.einsum('bhcd,bhde->bhce', wk_i, state)
        o = jnp.einsum('bhcd,bhde->bhce', qd_i, state) + jnp.einsum('bhij,bhje->bhie', a_i, v_new)
        state = state * gl_i[..., None, None] + jnp.einsum('bhcd,bhce->bhde', kd_i, v_new)
        return state, o

    mv = lambda a: jnp.moveaxis(a, 2, 0)
    state0 = jnp.zeros((B, H, Dk, Dv), jnp.float32)
    _, o = lax.scan(step, state0, (mv(q_dec), mv(k_dec), mv(u), mv(wk), mv(a_intra), mv(g_last)))
    return jnp.moveaxis(o, 0, 2).reshape(B, H, S, Dv)


def deltanet_branch(qkv, b_logit, a_logit, gate, conv_w, a_log, dt_bias, onorm_w):
    B, S, _ = qkv.shape
    qkv = jax.nn.silu(causal_dwconv(qkv, conv_w)).astype(jnp.float32)
    q, k, v = jnp.split(qkv, 3, axis=-1)
    heads = lambda a: a.reshape(B, S, DN_HEADS, DN_HEAD_DIM).transpose(0, 2, 1, 3)
    q, k, v = l2_norm(heads(q)), l2_norm(heads(k)), heads(v)
    beta = jax.nn.sigmoid(b_logit.astype(jnp.float32)).transpose(0, 2, 1)
    g = -(jnp.exp(a_log.astype(jnp.float32)) *
          jax.nn.softplus(a_logit.astype(jnp.float32) + dt_bias.astype(jnp.float32))).transpose(0, 2, 1)
    o = gated_delta_rule(q, k, v, g, beta).transpose(0, 2, 1, 3)
    o = o * lax.rsqrt(jnp.mean(o * o, axis=-1, keepdims=True) + EPS) * onorm_w.astype(jnp.float32)
    o = o * jax.nn.silu(gate.astype(jnp.float32).reshape(B, S, DN_HEADS, DN_HEAD_DIM))
    return o.reshape(B, S, DN_WIDTH).astype(qkv.dtype if False else gate.dtype)


def fox_branch(qkv, f_logit, f_bias):
    B, S, _ = qkv.shape
    q, k, v = jnp.split(qkv, 3, axis=-1)
    heads = lambda a: a.reshape(B, S, FOX_HEADS, FOX_HEAD_DIM).transpose(0, 2, 1, 3)
    q, k, v = heads(q), heads(k), heads(v)
    log_f = jax.nn.log_sigmoid(f_logit.astype(jnp.float32) + f_bias.astype(jnp.float32))
    cum_f = jnp.cumsum(log_f.transpose(0, 2, 1), axis=-1)
    scale = FOX_HEAD_DIM ** -0.5
    outs = []
    for i in range(S // QBLOCK):
        lo, hi = i * QBLOCK, (i + 1) * QBLOCK
        s = jnp.einsum('bhqd,bhkd->bhqk', q[:, :, lo:hi], k[:, :, :hi]).astype(jnp.float32) * scale
        s = s + cum_f[:, :, lo:hi, None] - cum_f[:, :, None, :hi]
        mask = (lo + jnp.arange(QBLOCK))[:, None] >= jnp.arange(hi)[None, :]
        p = jax.nn.softmax(jnp.where(mask, s, -jnp.inf), axis=-1)
        outs.append(jnp.einsum('bhqk,bhkd->bhqd', p.astype(v.dtype), v[:, :, :hi]))
    o = jnp.concatenate(outs, axis=2)
    return o.transpose(0, 2, 1, 3).reshape(B, S, FOX_WIDTH)


def hybrid_mixer(h, w_in, pool_w, pool_scale, dn_conv_w, dn_a_log, dn_dt_bias, dn_onorm_w,
                 fox_f_bias, w_branch_pool, w_branch_dn, w_branch_fox, w_out):
    B, S, D = h.shape
    proj = h @ w_in
    pool_in, dn_qkv, dn_b, dn_a, dn_g, fox_qkv, fox_f, gate_logits = jnp.split(proj, split_points(), axis=-1)
    y_pool = pool_mixer(pool_in, pool_w, pool_scale)
    y_dn = deltanet_branch(dn_qkv, dn_b, dn_a, dn_g, dn_conv_w, dn_a_log, dn_dt_bias, dn_onorm_w)
    y_fox = fox_branch(fox_qkv, fox_f, fox_f_bias)
    gates = jax.nn.sigmoid(gate_logits.astype(jnp.float32)).astype(h.dtype).reshape(B, S, N_BRANCH, D)
    merged = (gates[:, :, 0] * (y_pool @ w_branch_pool)
              + gates[:, :, 1] * (y_dn @ w_branch_dn)
              + gates[:, :, 2] * (y_fox @ w_branch_fox))
    return merged @ w_out


def peer_ffn(h, w_q, sub_keys, expert_u, expert_v):
    B, S, D = h.shape
    T = B * S
    hf = h.reshape(T, D)
    q = (hf @ w_q).reshape(T, PEER_HEADS, 2, PEER_HALF)
    scores = jnp.einsum('thpd,hpnd->thpn', q, sub_keys).astype(jnp.float32)
    top_s, top_i = lax.top_k(scores, PEER_TOPK)
    cand_s = top_s[:, :, 0, :, None] + top_s[:, :, 1, None, :]
    cand_id = top_i[:, :, 0, :, None] * N_SUBKEYS + top_i[:, :, 1, None, :]
    kk = PEER_TOPK * PEER_TOPK
    best_s, best_pos = lax.top_k(cand_s.reshape(T, PEER_HEADS, kk), PEER_TOPK)
    ids = jnp.take_along_axis(cand_id.reshape(T, PEER_HEADS, kk), best_pos, axis=-1)
    wts = jax.nn.softmax(best_s, axis=-1)
    n_slots = PEER_HEADS * PEER_TOPK
    nb = T // PEER_TOKEN_BLOCK

    def block(args):
        xb, idb, wb = args
        act = jax.nn.gelu(jnp.einsum('td,ted->te', xb, expert_u[idb]), approximate=False)
        coef = (act.astype(jnp.float32) * wb).astype(xb.dtype)
        return jnp.einsum('te,ted->td', coef, expert_v[idb])

    y = lax.map(block, (hf.reshape(nb, PEER_TOKEN_BLOCK, D),
                        ids.reshape(nb, PEER_TOKEN_BLOCK, n_slots),
                        wts.reshape(nb, PEER_TOKEN_BLOCK, n_slots)))
    return y.reshape(B, S, D)


def setup_inputs(seed: int = 0) -> dict:
    key = jax.random.key(seed)
    ks = jax.random.split(key, 24)
    f32 = jnp.float32
    nrm = lambda k, shape, s: jax.random.normal(k, shape, f32) * s
    L = DEPTH
    x = nrm(ks[0], (BATCH, SEQ, D_MODEL), 1.0)
    c = nrm(ks[1], (BATCH, D_MODEL), 1.0)
    ada_w = nrm(ks[2], (L, D_MODEL, 6 * D_MODEL), 0.5 * D_MODEL ** -0.5)
    ada_b = nrm(ks[3], (L, 6 * D_MODEL), 0.02)
    norm_mix_w = 1.0 + nrm(ks[4], (L, D_MODEL), 0.05)
    w_in = nrm(ks[5], (L, D_MODEL, IN_WIDTH), D_MODEL ** -0.5)
    pool_w = nrm(ks[6], (L, N_POOL, POOL_GROUP, POOL_GROUP), POOL_GROUP ** -0.5)
    pool_scale = jax.random.uniform(ks[7], (L, POOL_WIDTH), f32, 0.5, 1.5)
    dn_conv_w = nrm(ks[8], (L, CONV_K, 3 * DN_WIDTH), CONV_K ** -0.5)
    dn_a_log = jnp.log(jax.random.uniform(ks[9], (L, DN_HEADS), f32, 1.0, 16.0))
    dt = jnp.exp(jax.random.uniform(ks[10], (L, DN_HEADS), f32, math.log(1e-3), math.log(1e-1)))
    dn_dt_bias = dt + jnp.log(-jnp.expm1(-dt))
    dn_onorm_w = 1.0 + nrm(ks[11], (L, DN_HEAD_DIM), 0.05)
    fox_f_bias = jax.random.uniform(ks[12], (L, FOX_HEADS), f32, 1.0, 5.0)
    w_branch_pool = nrm(ks[13], (L, POOL_WIDTH, D_MODEL), POOL_WIDTH ** -0.5)
    w_branch_dn = nrm(ks[14], (L, DN_WIDTH, D_MODEL), DN_WIDTH ** -0.5)
    w_branch_fox = nrm(ks[15], (L, FOX_WIDTH, D_MODEL), FOX_WIDTH ** -0.5)
    w_out = nrm(ks[16], (L, D_MODEL, D_MODEL), D_MODEL ** -0.5)
    norm_ffn_w = 1.0 + nrm(ks[17], (L, D_MODEL), 0.05)
    peer_w_q = nrm(ks[18], (L, D_MODEL, PEER_HEADS * PEER_KEY_DIM), D_MODEL ** -0.5)
    peer_sub_keys = nrm(ks[19], (L, PEER_HEADS, 2, N_SUBKEYS, PEER_HALF), PEER_HALF ** -0.5)
    peer_u = nrm(ks[20], (L, N_EXPERTS, D_MODEL), D_MODEL ** -0.5)
    peer_v = nrm(ks[21], (L, N_EXPERTS, D_MODEL), PEER_HEADS ** -0.5)
    final_norm_w = 1.0 + nrm(ks[22], (D_MODEL,), 0.05)
    return {'x': x, 'c': c, 'ada_w': ada_w, 'ada_b': ada_b, 'norm_mix_w': norm_mix_w, 'w_in': w_in,
            'pool_w': pool_w, 'pool_scale': pool_scale, 'dn_conv_w': dn_conv_w, 'dn_a_log': dn_a_log,
            'dn_dt_bias': dn_dt_bias, 'dn_onorm_w': dn_onorm_w, 'fox_f_bias': fox_f_bias,
            'w_branch_pool': w_branch_pool, 'w_branch_dn': w_branch_dn, 'w_branch_fox': w_branch_fox,
            'w_out': w_out, 'norm_ffn_w': norm_ffn_w, 'peer_w_q': peer_w_q, 'peer_sub_keys': peer_sub_keys,
            'peer_u': peer_u, 'peer_v': peer_v, 'final_norm_w': final_norm_w}


def reference(x, c, ada_w, ada_b, norm_mix_w, w_in, pool_w, pool_scale, dn_conv_w, dn_a_log,
              dn_dt_bias, dn_onorm_w, fox_f_bias, w_branch_pool, w_branch_dn, w_branch_fox,
              w_out, norm_ffn_w, peer_w_q, peer_sub_keys, peer_u, peer_v, final_norm_w):
    for l in range(DEPTH):
        mod = (jax.nn.silu(c) @ ada_w[l] + ada_b[l])[:, None, :]
        sh_m, sc_m, g_m, sh_f, sc_f, g_f = jnp.split(mod, 6, axis=-1)
        h = rms_norm(x, norm_mix_w[l]) * (1 + sc_m) + sh_m
        x = x + g_m * hybrid_mixer(h, w_in[l], pool_w[l], pool_scale[l], dn_conv_w[l], dn_a_log[l],
                                   dn_dt_bias[l], dn_onorm_w[l], fox_f_bias[l], w_branch_pool[l],
                                   w_branch_dn[l], w_branch_fox[l], w_out[l])
        h = rms_norm(x, norm_ffn_w[l]) * (1 + sc_f) + sh_f
        x = x + g_f * peer_ffn(h, peer_w_q[l], peer_sub_keys[l], peer_u[l], peer_v[l])
    return rms_norm(x, final_norm_w)
```

```python
import functools

import jax
import jax.numpy as jnp
from jax import lax
from jax.experimental import pallas as pl
from jax.experimental.pallas import tpu as pltpu

F32 = jnp.float32
BF16 = jnp.bfloat16
EPS = 1e-6
MIB = 1024 * 1024

SUBLANES = 8
LANES = 128

CHUNK = 64
CONV_K = 4
POOL_WINDOWS = (2, 4, 8, 16)
POOL_HALO = 16
N_HEADS = 8
HEAD_DIM = 128
PEER_HEADS = 8
N_SUBKEYS = 128
PEER_TOPK = 16
NEG_INF = float("-inf")


def _params(semantics, vmem_mib):
    return pltpu.CompilerParams(dimension_semantics=semantics,
                                vmem_limit_bytes=vmem_mib * MIB)


def _dot(a, b):
    return jnp.dot(a, b, preferred_element_type=F32)


def _dot_nt(a, b):
    return lax.dot_general(a, b, (((1,), (1,)), ((), ())), preferred_element_type=F32)


def _dot_tn(a, b):
    return lax.dot_general(a, b, (((0,), (0,)), ((), ())), preferred_element_type=F32)


def _dot01(m01, x):
    n = x.shape[1]
    hi = x.astype(BF16)
    r1 = x - hi.astype(F32)
    mid = r1.astype(BF16)
    lo = (r1 - mid.astype(F32)).astype(BF16)
    y = _dot(m01, jnp.concatenate([hi, mid, lo], axis=1))
    return (y[:, :n] + y[:, n:2 * n]) + y[:, 2 * n:]


def _sigmoid(x):
    return 1.0 / (1.0 + jnp.exp(-x))


def _softplus(x):
    return jnp.maximum(x, 0.0) + jnp.log1p(jnp.exp(-jnp.abs(x)))


def _ada_body(c_ref, w_ref, b_ref, o_ref):
    c = c_ref[...]
    cs = (c * _sigmoid(c)).astype(BF16)
    o_ref[0] = _dot(cs, w_ref[0].astype(BF16)) + b_ref[0]


def _ada(c8, ada_w, ada_b, tn=1024):
    depth, d, n = ada_w.shape
    return pl.pallas_call(
        _ada_body,
        grid=(depth, n // tn),
        in_specs=[pl.BlockSpec((SUBLANES, d), lambda l, j: (0, 0)),
                  pl.BlockSpec((1, d, tn), lambda l, j: (l, 0, j)),
                  pl.BlockSpec((1, 1, tn), lambda l, j: (l, 0, j))],
        out_specs=pl.BlockSpec((1, SUBLANES, tn), lambda l, j: (l, 0, j)),
        out_shape=jax.ShapeDtypeStruct((depth, SUBLANES, n), F32),
        compiler_params=_params(("arbitrary", "arbitrary"), 32),
        name="ada_mod",
    )(c8, ada_w, ada_b.reshape(depth, 1, n))


def _rms(x, w):
    ms = jnp.mean(x * x, axis=-1, keepdims=True)
    return x * lax.rsqrt(ms + EPS) * w


def _norm_mod_body(x_ref, w_ref, sc_ref, sh_ref, o_ref, *maybe_t_ref):
    y = _rms(x_ref[0], w_ref[...]) * (1.0 + sc_ref[0]) + sh_ref[0]
    o_ref[0] = y.astype(o_ref.dtype)
    if maybe_t_ref:
        maybe_t_ref[0][...] = y.T.astype(BF16)


def _norm_mod(x, w, sc, sh, *, transposed_copy=False, ts=512):
    b, s, d = x.shape
    ts = min(ts, s)
    ns = s // ts
    out_shape = [jax.ShapeDtypeStruct((b, s, d), BF16)]
    out_specs = [pl.BlockSpec((1, ts, d), lambda i, j: (i, j, 0))]
    if transposed_copy:
        out_shape.append(jax.ShapeDtypeStruct((d, b * s), BF16))
        out_specs.append(pl.BlockSpec((d, ts), lambda i, j: (0, i * ns + j)))
    res = pl.pallas_call(
        _norm_mod_body,
        grid=(b, ns),
        in_specs=[pl.BlockSpec((1, ts, d), lambda i, j: (i, j, 0)),
                  pl.BlockSpec((1, d), lambda i, j: (0, 0)),
                  pl.BlockSpec((1, 1, d), lambda i, j: (i, 0, 0)),
                  pl.BlockSpec((1, 1, d), lambda i, j: (i, 0, 0))],
        out_specs=out_specs,
        out_shape=out_shape,
        compiler_params=_params(("arbitrary", "arbitrary"), 40),
        name="norm_mod",
    )(x, w.reshape(1, d), sc, sh)
    return res if transposed_copy else res[0]


def _final_norm_body(x_ref, w_ref, o_ref):
    o_ref[0] = _rms(x_ref[0], w_ref[...])


def _final_norm(x, w, ts=512):
    b, s, d = x.shape
    ts = min(ts, s)
    return pl.pallas_call(
        _final_norm_body,
        grid=(b, s // ts),
        in_specs=[pl.BlockSpec((1, ts, d), lambda i, j: (i, j, 0)),
                  pl.BlockSpec((1, d), lambda i, j: (0, 0))],
        out_specs=pl.BlockSpec((1, ts, d), lambda i, j: (i, j, 0)),
        out_shape=jax.ShapeDtypeStruct((b, s, d), F32),
        compiler_params=_params(("arbitrary", "arbitrary"), 32),
        name="final_norm",
    )(x, w.reshape(1, d))


def _mm_body(a_ref, w_ref, o_ref, *, sigmoid):
    acc = _dot(a_ref[...], w_ref[...])
    if sigmoid:
        acc = _sigmoid(acc)
    o_ref[...] = acc.astype(o_ref.dtype)


def _matmul(a, w, out_dtype, *, sigmoid=False, tm=1024, tn=1024):
    m, k = a.shape
    n = w.shape[1]
    tm, tn = min(tm, m), min(tn, n)
    return pl.pallas_call(
        functools.partial(_mm_body, sigmoid=sigmoid),
        grid=(m // tm, n // tn),
        in_specs=[pl.BlockSpec((tm, k), lambda i, j: (i, 0)),
                  pl.BlockSpec((k, tn), lambda i, j: (0, j))],
        out_specs=pl.BlockSpec((tm, tn), lambda i, j: (i, j)),
        out_shape=jax.ShapeDtypeStruct((m, n), out_dtype),
        compiler_params=_params(("arbitrary", "arbitrary"), 44),
        name="matmul",
    )(a, w)


def _pool_body(p_ref, w_ref, scale_ref, o_ref, ext_ref, *, ts, group):
    j = pl.program_id(1)

    @pl.when(j == 0)
    def _():
        ext_ref[0:POOL_HALO, :] = jnp.zeros((POOL_HALO, ext_ref.shape[1]), F32)

    ext_ref[POOL_HALO:POOL_HALO + ts, :] = p_ref[0]
    pos = (j * ts + lax.broadcasted_iota(jnp.int32, (ts, 1), 0) + 1).astype(F32)
    for g, win in enumerate(POOL_WINDOWS):
        cols = slice(g * group, (g + 1) * group)
        cur = ext_ref[POOL_HALO:POOL_HALO + ts, cols]
        acc = cur
        for back in range(1, win):
            acc = acc + ext_ref[POOL_HALO - back:POOL_HALO - back + ts, cols]
        pooled = acc / jnp.minimum(pos, float(win)) - cur
        y = _dot(pooled.astype(BF16), w_ref[g])
        o_ref[0, :, cols] = (y * scale_ref[:, cols]).astype(o_ref.dtype)
    ext_ref[0:POOL_HALO, :] = ext_ref[ts:ts + POOL_HALO, :]


def _pool(proj_f32, pool_w, pool_scale, b, s, ts=512):
    n_pool, group, _ = pool_w.shape
    width = n_pool * group
    ts = min(ts, s)
    return pl.pallas_call(
        functools.partial(_pool_body, ts=ts, group=group),
        grid=(b, s // ts),
        in_specs=[pl.BlockSpec((1, ts, width), lambda i, j: (i, j, 0)),
                  pl.BlockSpec((n_pool, group, group), lambda i, j: (0, 0, 0)),
                  pl.BlockSpec((1, width), lambda i, j: (0, 0))],
        out_specs=pl.BlockSpec((1, ts, width), lambda i, j: (i, j, 0)),
        out_shape=jax.ShapeDtypeStruct((b, s, width), BF16),
        scratch_shapes=[pltpu.VMEM((ts + POOL_HALO, width), F32)],
        compiler_params=_params(("arbitrary", "arbitrary"), 32),
        name="pool_mixer",
    )(proj_f32, pool_w, pool_scale.reshape(1, width))


def _cumf_body(x_ref, bias_ref, o_ref, carry_ref, *, ts):
    @pl.when(pl.program_id(1) == 0)
    def _():
        carry_ref[...] = jnp.zeros_like(carry_ref)

    z = x_ref[0] + bias_ref[...]
    log_f = jnp.minimum(z, 0.0) - jnp.log1p(jnp.exp(-jnp.abs(z)))
    row = lax.broadcasted_iota(jnp.int32, (ts, ts), 0)
    col = lax.broadcasted_iota(jnp.int32, (ts, ts), 1)
    tril = jnp.where(row >= col, 1.0, 0.0).astype(BF16)
    cs = _dot01(tril, log_f) + carry_ref[...]
    o_ref[0] = cs
    carry_ref[...] = cs[ts - 1:ts, :]


def _cumf(small, bias_row, ts=512):
    b, s, n = small.shape
    ts = min(ts, s)
    return pl.pallas_call(
        functools.partial(_cumf_body, ts=ts),
        grid=(b, s // ts),
        in_specs=[pl.BlockSpec((1, ts, n), lambda i, j: (i, j, 0)),
                  pl.BlockSpec((1, n), lambda i, j: (0, 0))],
        out_specs=pl.BlockSpec((1, ts, n), lambda i, j: (i, j, 0)),
        out_shape=jax.ShapeDtypeStruct((b, s, n), F32),
        scratch_shapes=[pltpu.VMEM((1, n), F32)],
        compiler_params=_params(("arbitrary", "arbitrary"), 32),
        name="cum_forget",
    )(small, bias_row)


def _dn_body(q_ref, k_ref, v_ref, wq_ref, wk_ref, wv_ref, b_ref, a_ref, alog_ref, dtb_ref,
             onw_ref, gate_ref, o_ref, state_ref, eq_ref, ek_ref, ev_ref, *, ts):
    halo = SUBLANES

    @pl.when(pl.program_id(2) == 0)
    def _():
        state_ref[...] = jnp.zeros_like(state_ref)
        for e_ref in (eq_ref, ek_ref, ev_ref):
            e_ref[0:halo, :] = jnp.zeros((halo, HEAD_DIM), F32)

    def conv_silu(x_ref, e_ref, w_ref):
        e_ref[halo:halo + ts, :] = x_ref[0]
        w = w_ref[...]
        y = e_ref[halo:halo + ts, :] * w[CONV_K - 1:CONV_K, :]
        for back in range(1, CONV_K):
            tap = CONV_K - 1 - back
            y = y + e_ref[halo - back:halo - back + ts, :] * w[tap:tap + 1, :]
        e_ref[0:halo, :] = e_ref[ts:ts + halo, :]
        return y * _sigmoid(y)

    def l2n(x):
        return x * lax.rsqrt(jnp.sum(x * x, axis=-1, keepdims=True) + EPS)

    q_all = l2n(conv_silu(q_ref, eq_ref, wq_ref)) * (HEAD_DIM ** -0.5)
    k_all = l2n(conv_silu(k_ref, ek_ref, wk_ref))
    v_all = conv_silu(v_ref, ev_ref, wv_ref)
    beta_all = _sigmoid(b_ref[0, 0])
    g_all = -jnp.exp(alog_ref[0]) * _softplus(a_ref[0, 0] + dtb_ref[0])

    ri = lax.broadcasted_iota(jnp.int32, (CHUNK, CHUNK), 0)
    ci = lax.broadcasted_iota(jnp.int32, (CHUNK, CHUNK), 1)
    incl = ri >= ci
    strict = ri > ci
    eye = jnp.where(ri == ci, 1.0, 0.0)
    tril01 = jnp.where(incl, 1.0, 0.0).astype(BF16)
    triu = jnp.where(ri <= ci, 1.0, 0.0)
    ones01 = jnp.ones((CHUNK, CHUNK), BF16)
    onw = onw_ref[...]

    state = state_ref[...]
    for c in range(ts // CHUNK):
        rows = slice(c * CHUNK, (c + 1) * CHUNK)
        q, k, v = q_all[rows], k_all[rows], v_all[rows]
        beta = beta_all[rows]
        gb = jnp.broadcast_to(g_all[rows], (CHUNK, HEAD_DIM))
        gcum = _dot01(tril01, gb)
        gcum_row = _dot01(ones01, gb[:, :CHUNK] * triu)
        decay = jnp.exp(jnp.where(incl, gcum[:, :CHUNK] - gcum_row, NEG_INF))
        kb = k * beta
        vb = v * beta
        qk = _dot_nt(jnp.concatenate([q, kb], axis=0).astype(BF16), k.astype(BF16))
        a_intra = qk[:CHUNK] * decay
        x = -jnp.where(strict, qk[CHUNK:] * decay, 0.0)
        t_mat = eye + x
        p = _dot(x.astype(BF16), x.astype(BF16))
        for _ in range(4):
            tp = _dot(jnp.concatenate([t_mat, p], axis=0).astype(BF16), p.astype(BF16))
            t_mat = t_mat + tp[:CHUNK]
            p = tp[CHUNK:]
        t_mat = t_mat + _dot(t_mat.astype(BF16), p.astype(BF16))
        eg = jnp.exp(gcum)
        uw = _dot(t_mat.astype(BF16), jnp.concatenate([vb, kb * eg], axis=1).astype(BF16))
        u, wk = uw[:, :HEAD_DIM], uw[:, HEAD_DIM:]
        g_last = gcum[CHUNK - 1:CHUNK, :]
        q_dec = q * eg
        k_dec = k * jnp.exp(g_last - gcum)
        ws = _dot(jnp.concatenate([wk, q_dec], axis=0).astype(BF16), state.astype(BF16))
        v_new = u - ws[:CHUNK]
        o = ws[CHUNK:] + _dot(a_intra.astype(BF16), v_new.astype(BF16))
        state = state * jnp.exp(g_last) + _dot_tn(k_dec.astype(BF16), v_new.astype(BF16))
        o = o * lax.rsqrt(jnp.mean(o * o, axis=-1, keepdims=True) + EPS) * onw
        gate = gate_ref[0, rows, :].astype(F32)
        o_ref[0, rows, :] = (o * (gate * _sigmoid(gate))).astype(o_ref.dtype)
    state_ref[...] = state


def _deltanet(proj_f32, qkv_col0, proj_bf16, gate_col0, b_col, a_col, conv_w, a_log, dt_bias,
              onorm_w, b, s, ts=256):
    ts = min(ts, s)
    width = N_HEADS * HEAD_DIM

    def col_spec(block0):
        return pl.BlockSpec((1, ts, HEAD_DIM), lambda i, h, j: (i, j, block0 + h))

    def conv_spec(block0):
        return pl.BlockSpec((CONV_K, HEAD_DIM), lambda i, h, j: (0, block0 + h))

    column = pl.BlockSpec((1, 1, ts, 1), lambda i, h, j: (i, h, j, 0))
    scalar = pl.BlockSpec((1, 1, 1), lambda i, h, j: (h, 0, 0))
    return pl.pallas_call(
        functools.partial(_dn_body, ts=ts),
        grid=(b, N_HEADS, s // ts),
        in_specs=[col_spec(qkv_col0), col_spec(qkv_col0 + N_HEADS), col_spec(qkv_col0 + 2 * N_HEADS),
                  conv_spec(0), conv_spec(N_HEADS), conv_spec(2 * N_HEADS),
                  column, column, scalar, scalar,
                  pl.BlockSpec((1, HEAD_DIM), lambda i, h, j: (0, 0)),
                  col_spec(gate_col0)],
        out_specs=pl.BlockSpec((1, ts, HEAD_DIM), lambda i, h, j: (i, j, h)),
        out_shape=jax.ShapeDtypeStruct((b, s, width), BF16),
        scratch_shapes=[pltpu.VMEM((HEAD_DIM, HEAD_DIM), F32)] +
                       [pltpu.VMEM((ts + SUBLANES, HEAD_DIM), F32)] * 3,
        compiler_params=_params(("arbitrary", "arbitrary", "arbitrary"), 32),
        name="deltanet",
    )(proj_f32, proj_f32, proj_f32, conv_w, conv_w, conv_w, b_col, a_col,
      a_log.reshape(N_HEADS, 1, 1), dt_bias.reshape(N_HEADS, 1, 1),
      onorm_w.reshape(1, HEAD_DIM), proj_bf16)


def _fox_body(q_ref, k_ref, v_ref, cq_ref, ck_ref, o_ref, *, tq):
    qi = pl.program_id(2)
    q = q_ref[0]
    cq = cq_ref[0, 0]
    scale = HEAD_DIM ** -0.5
    row = qi * tq + lax.broadcasted_iota(jnp.int32, (tq, tq), 0)
    col0 = lax.broadcasted_iota(jnp.int32, (tq, tq), 1)

    def step(ki, carry):
        m, l, acc = carry
        start = pl.multiple_of(ki * tq, tq)
        k = k_ref[0, pl.ds(start, tq), :]
        v = v_ref[0, pl.ds(start, tq), :]
        s = _dot_nt(q, k) * scale + cq - ck_ref[0, 0, :, pl.ds(start, tq)]
        s = jnp.where(col0 + ki * tq <= row, s, NEG_INF)
        m_new = jnp.maximum(m, jnp.max(s, axis=-1, keepdims=True))
        p = jnp.exp(s - m_new)
        alpha = jnp.exp(m - m_new)
        l = alpha * l + jnp.sum(p, axis=-1, keepdims=True)
        acc = alpha * acc + _dot(p.astype(BF16), v)
        return m_new, l, acc

    init = (jnp.full((tq, 1), NEG_INF, F32), jnp.zeros((tq, 1), F32),
            jnp.zeros((tq, HEAD_DIM), F32))
    _, l, acc = lax.fori_loop(0, qi + 1, step, init)
    o_ref[0] = (acc / l).astype(o_ref.dtype)


def _fox(proj_bf16, qkv_col0, cum_q, cum_k, b, s, tq=512):
    tq = min(tq, s)
    width = N_HEADS * HEAD_DIM
    return pl.pallas_call(
        functools.partial(_fox_body, tq=tq),
        grid=(b, N_HEADS, s // tq),
        in_specs=[pl.BlockSpec((1, tq, HEAD_DIM), lambda i, h, j: (i, j, qkv_col0 + h)),
                  pl.BlockSpec((1, s, HEAD_DIM), lambda i, h, j: (i, 0, qkv_col0 + N_HEADS + h)),
                  pl.BlockSpec((1, s, HEAD_DIM), lambda i, h, j: (i, 0, qkv_col0 + 2 * N_HEADS + h)),
                  pl.BlockSpec((1, 1, tq, 1), lambda i, h, j: (i, h, j, 0)),
                  pl.BlockSpec((1, 1, 1, s), lambda i, h, j: (i, h, 0, 0))],
        out_specs=pl.BlockSpec((1, tq, HEAD_DIM), lambda i, h, j: (i, j, h)),
        out_shape=jax.ShapeDtypeStruct((b, s, width), BF16),
        compiler_params=_params(("arbitrary", "arbitrary", "arbitrary"), 32),
        name="fox_attention",
    )(proj_bf16, proj_bf16, proj_bf16, cum_q, cum_k)


def _merge_body(yp_ref, yd_ref, yf_ref, wp_ref, wd_ref, wf_ref, g0_ref, g1_ref, g2_ref, o_ref):
    acc = g0_ref[...].astype(F32) * _dot(yp_ref[...], wp_ref[...])
    acc = acc + g1_ref[...].astype(F32) * _dot(yd_ref[...], wd_ref[...])
    acc = acc + g2_ref[...].astype(F32) * _dot(yf_ref[...], wf_ref[...])
    o_ref[...] = acc.astype(o_ref.dtype)


def _merge(y_pool, y_dn, y_fox, w_pool, w_dn, w_fox, gates, tm=512, tn=1024):
    m, k = y_pool.shape
    n = w_pool.shape[1]
    tm = min(tm, m)
    nb = n // tn
    y_spec = pl.BlockSpec((tm, k), lambda i, j: (i, 0))
    w_spec = pl.BlockSpec((k, tn), lambda i, j: (0, j))

    def gate_spec(branch):
        return pl.BlockSpec((tm, tn), lambda i, j: (i, branch * nb + j))

    return pl.pallas_call(
        _merge_body,
        grid=(m // tm, nb),
        in_specs=[y_spec, y_spec, y_spec, w_spec, w_spec, w_spec,
                  gate_spec(0), gate_spec(1), gate_spec(2)],
        out_specs=pl.BlockSpec((tm, tn), lambda i, j: (i, j)),
        out_shape=jax.ShapeDtypeStruct((m, n), BF16),
        compiler_params=_params(("arbitrary", "arbitrary"), 44),
        name="branch_merge",
    )(y_pool, y_dn, y_fox, w_pool, w_dn, w_fox, gates, gates, gates)


def _out_proj_body(a_ref, w_ref, x_ref, g_ref, o_ref):
    o_ref[0] = x_ref[0] + g_ref[0] * _dot(a_ref[0], w_ref[...])


def _out_proj(merged, w_out, x, gate, tm=512, tn=1024):
    b, s, d = x.shape
    k = merged.shape[-1]
    tm = min(tm, s)
    return pl.pallas_call(
        _out_proj_body,
        grid=(b, s // tm, d // tn),
        in_specs=[pl.BlockSpec((1, tm, k), lambda i, j, n: (i, j, 0)),
                  pl.BlockSpec((k, tn), lambda i, j, n: (0, n)),
                  pl.BlockSpec((1, tm, tn), lambda i, j, n: (i, j, n)),
                  pl.BlockSpec((1, 1, tn), lambda i, j, n: (i, 0, n))],
        out_specs=pl.BlockSpec((1, tm, tn), lambda i, j, n: (i, j, n)),
        out_shape=jax.ShapeDtypeStruct((b, s, d), F32),
        compiler_params=_params(("arbitrary", "arbitrary", "arbitrary"), 40),
        name="out_proj_residual",
    )(merged, w_out, x, gate)


def _peer_select_body(h_ref, wq_ref, sk_ref, e1_ref, cnt_ref, e2_ref, rank2_ref,
                      sc_ref, rk_ref, tp_ref, *, tt):
    n_lt = tt // LANES
    n_hp = 2 * PEER_HEADS
    q = _dot(h_ref[...], wq_ref[...])
    for hp in range(n_hp):
        q_hp = q[:, hp * N_SUBKEYS:(hp + 1) * N_SUBKEYS].astype(BF16)
        s_t = _dot_nt(sk_ref[hp], q_hp)
        for lt in range(n_lt):
            sc_ref[hp * n_lt + lt] = s_t[:, lt * LANES:(lt + 1) * LANES]

    key_id = lax.broadcasted_iota(jnp.int32, (N_SUBKEYS, LANES), 0).astype(F32)

    def half_topk(idx, carry):
        v = sc_ref[idx]
        rank = jnp.full((N_SUBKEYS, LANES), float(PEER_TOPK), F32)
        tops = []
        for r in range(PEER_TOPK):
            m = jnp.max(v, axis=0, keepdims=True)
            first = jnp.min(jnp.where(v == m, key_id, float(N_SUBKEYS)), axis=0, keepdims=True)
            hit = key_id == first
            rank = jnp.where(hit, float(r), rank)
            v = jnp.where(hit, NEG_INF, v)
            tops.append(m)
        rk_ref[idx] = rank
        tp_ref[idx] = jnp.concatenate(tops, axis=0)
        return carry

    lax.fori_loop(0, n_hp * n_lt, half_topk, 0)

    sub8 = lax.broadcasted_iota(jnp.int32, (SUBLANES, LANES), 0).astype(F32)
    sub16 = lax.broadcasted_iota(jnp.int32, (PEER_TOPK, LANES), 0).astype(F32)

    def pair_topk(it, carry):
        head = it // n_lt
        lt = it % n_lt
        i1 = (2 * head) * n_lt + lt
        i2 = (2 * head + 1) * n_lt + lt
        t1 = tp_ref[i1]
        t2 = tp_ref[i2]
        blocks = [t1[0:1] + t2, t1[1:2] + t2[0:SUBLANES]]
        poss = [sub16, PEER_TOPK + sub8]
        for a in range(2, SUBLANES):
            nb = PEER_TOPK // (a + 1)
            blocks.append(jnp.where(sub8 < nb, t1[a:a + 1] + t2[0:SUBLANES], NEG_INF))
            poss.append(a * PEER_TOPK + sub8)
        blocks.append(t1[SUBLANES:PEER_TOPK] + t2[0:1])
        poss.append((SUBLANES + sub8) * PEER_TOPK)
        sels = [jnp.zeros_like(blk) for blk in blocks]
        best = t1[0:1] + t2[0:1]
        z = jnp.zeros((1, LANES), F32)
        for r in range(PEER_TOPK):
            m = functools.reduce(jnp.maximum, [jnp.max(blk, axis=0, keepdims=True) for blk in blocks])
            first = functools.reduce(jnp.minimum, [
                jnp.min(jnp.where(blk == m, pos, 1e9), axis=0, keepdims=True)
                for blk, pos in zip(blocks, poss)])
            hits = [pos == first for pos in poss]
            sels = [jnp.where(hit, 1.0, sel) for hit, sel in zip(hits, sels)]
            blocks = [jnp.where(hit, NEG_INF, blk) for hit, blk in zip(hits, blocks)]
            z = z + jnp.exp(m - best)
        cnts = [jnp.sum(sels[a], axis=0, keepdims=True) for a in range(SUBLANES)]
        cnts += [sels[SUBLANES][a:a + 1] for a in range(SUBLANES)]
        rank1 = rk_ref[i1]
        cnt = jnp.zeros((N_SUBKEYS, LANES), F32)
        for a in range(PEER_TOPK):
            cnt = jnp.where(rank1 == float(a), cnts[a], cnt)
        e1_ref[head, lt] = jnp.exp(sc_ref[i1] - t1[0:1]) / z
        cnt_ref[head, lt] = cnt
        e2_ref[head, lt] = jnp.exp(sc_ref[i2] - t2[0:1])
        rank2_ref[head, lt] = rk_ref[i2]
        return carry

    lax.fori_loop(0, PEER_HEADS * n_lt, pair_topk, 0)


def _peer_select(h, w_q, sub_keys, tt=512):
    t, d = h.shape
    tt = min(tt, t)
    n_lt = tt // LANES
    n_hp = 2 * PEER_HEADS
    shape = jax.ShapeDtypeStruct((PEER_HEADS, t // LANES, N_SUBKEYS, LANES), F32)
    spec = pl.BlockSpec((PEER_HEADS, n_lt, N_SUBKEYS, LANES), lambda i: (0, i, 0, 0))
    return pl.pallas_call(
        functools.partial(_peer_select_body, tt=tt),
        grid=(t // tt,),
        in_specs=[pl.BlockSpec((tt, d), lambda i: (i, 0)),
                  pl.BlockSpec(w_q.shape, lambda i: (0, 0)),
                  pl.BlockSpec(sub_keys.shape, lambda i: (0, 0, 0))],
        out_specs=[spec] * 4,
        out_shape=[shape] * 4,
        scratch_shapes=[pltpu.VMEM((n_hp * n_lt, N_SUBKEYS, LANES), F32),
                        pltpu.VMEM((n_hp * n_lt, N_SUBKEYS, LANES), F32),
                        pltpu.VMEM((n_hp * n_lt, PEER_TOPK, LANES), F32)],
        compiler_params=_params(("arbitrary",), 48),
        name="peer_select",
    )(h, w_q, sub_keys)


def _peer_dense_body(u_ref, ht_ref, vt_ref, e1_ref, cnt_ref, e2_ref, rank2_ref, o_ref,
                     act_ref, coef_ref, *, tt, te):
    n_lt = tt // LANES
    n_i = te // N_SUBKEYS

    @pl.when(pl.program_id(1) == 0)
    def _():
        o_ref[...] = jnp.zeros_like(o_ref)

    act_ref[...] = _dot(u_ref[...], ht_ref[...])

    def first_key(ii, carry):
        rows = pl.ds(pl.multiple_of(ii * N_SUBKEYS, N_SUBKEYS), N_SUBKEYS)
        for lt in range(n_lt):
            lanes = slice(lt * LANES, (lt + 1) * LANES)
            w = jnp.zeros((N_SUBKEYS, LANES), F32)
            for head in range(PEER_HEADS):
                cnt = cnt_ref[head, lt, pl.ds(ii, 1), :]
                e1 = e1_ref[head, lt, pl.ds(ii, 1), :]
                w = w + jnp.where(rank2_ref[head, lt] < cnt, e2_ref[head, lt], 0.0) * e1
            a = act_ref[rows, lanes]
            gelu = 0.5 * a * (1.0 + lax.erf(a * (2.0 ** -0.5)))
            coef_ref[rows, lanes] = (gelu * w).astype(BF16)
        return carry

    lax.fori_loop(0, n_i, first_key, 0)
    o_ref[...] += _dot(vt_ref[...], coef_ref[...])


def _peer_dense(u, h_t, v_t, e1, cnt, e2, rank2, tt=512, te=1024):
    n_exp, d = u.shape
    t = h_t.shape[1]
    tt = min(tt, t)
    n_lt = tt // LANES
    n_i = te // N_SUBKEYS
    full = pl.BlockSpec((PEER_HEADS, n_lt, N_SUBKEYS, LANES), lambda i, j: (0, i, 0, 0))
    rows = pl.BlockSpec((PEER_HEADS, n_lt, n_i, LANES), lambda i, j: (0, i, j, 0))
    return pl.pallas_call(
        functools.partial(_peer_dense_body, tt=tt, te=te),
        grid=(t // tt, n_exp // te),
        in_specs=[pl.BlockSpec((te, d), lambda i, j: (j, 0)),
                  pl.BlockSpec((d, tt), lambda i, j: (0, i)),
                  pl.BlockSpec((d, te), lambda i, j: (0, j)),
                  rows, rows, full, full],
        out_specs=pl.BlockSpec((d, tt), lambda i, j: (0, i)),
        out_shape=jax.ShapeDtypeStruct((d, t), F32),
        scratch_shapes=[pltpu.VMEM((te, tt), F32), pltpu.VMEM((te, tt), BF16)],
        compiler_params=_params(("arbitrary", "arbitrary"), 52),
        name="peer_dense",
    )(u, h_t, v_t, e1, cnt, e2, rank2)


def _peer_residual_body(x_ref, yt_ref, g_ref, o_ref):
    o_ref[0] = x_ref[0] + g_ref[0] * yt_ref[...].T


def _peer_residual(x, y_t, gate, ts=512):
    b, s, d = x.shape
    ts = min(ts, s)
    ns = s // ts
    return pl.pallas_call(
        _peer_residual_body,
        grid=(b, ns),
        in_specs=[pl.BlockSpec((1, ts, d), lambda i, j: (i, j, 0)),
                  pl.BlockSpec((d, ts), lambda i, j: (0, i * ns + j)),
                  pl.BlockSpec((1, 1, d), lambda i, j: (i, 0, 0))],
        out_specs=pl.BlockSpec((1, ts, d), lambda i, j: (i, j, 0)),
        out_shape=jax.ShapeDtypeStruct((b, s, d), F32),
        compiler_params=_params(("arbitrary", "arbitrary"), 40),
        name="peer_residual",
    )(x, y_t, gate)


def kernel(x, c, ada_w, ada_b, norm_mix_w, w_in, pool_w, pool_scale, dn_conv_w, dn_a_log,
           dn_dt_bias, dn_onorm_w, fox_f_bias, w_branch_pool, w_branch_dn, w_branch_fox,
           w_out, norm_ffn_w, peer_w_q, peer_sub_keys, peer_u, peer_v, final_norm_w):
    b, s, d = x.shape
    t = b * s
    depth = ada_w.shape[0]
    pool_width = pool_w.shape[1] * pool_w.shape[2]
    width = N_HEADS * HEAD_DIM
    nh = N_HEADS

    c8 = jnp.zeros((SUBLANES, d), F32).at[:b].set(c)
    mod = _ada(c8, ada_w, ada_b)[:, :b].reshape(depth, b, 6, 1, d)

    o_pool = 0
    o_dnqkv = o_pool + pool_width
    o_dnb = o_dnqkv + 3 * width
    o_dna = o_dnb + nh
    o_dng = o_dna + nh
    o_foxqkv = o_dng + width
    o_foxf = o_foxqkv + 3 * width
    o_gates = o_foxf + nh

    for l in range(depth):
        sh_m, sc_m, g_m, sh_f, sc_f, g_f = (mod[l, :, i] for i in range(6))
        wl = w_in[l]
        w_f32 = wl[:, o_pool:o_dnb].astype(BF16)
        w_b16 = jnp.concatenate([wl[:, o_dng:o_foxqkv], wl[:, o_foxqkv:o_foxf]], axis=1).astype(BF16)
        w_gate = wl[:, o_gates:].astype(BF16)
        w_small = jnp.concatenate(
            [wl[:, o_dnb:o_dng], wl[:, o_foxf:o_gates],
             jnp.zeros((d, LANES - 3 * nh), F32)], axis=1).astype(BF16)

        h = _norm_mod(x, norm_mix_w[l], sc_m, sh_m).reshape(t, d)
        proj_f32 = _matmul(h, w_f32, F32).reshape(b, s, -1)
        proj_b16 = _matmul(h, w_b16, BF16).reshape(b, s, -1)
        gates = _matmul(h, w_gate, BF16, sigmoid=True)
        small = _matmul(h, w_small, F32).reshape(b, s, LANES)

        y_pool = _pool(proj_f32, pool_w[l].astype(BF16), pool_scale[l], b, s)

        cols = small.transpose(0, 2, 1)
        b_col = cols[:, 0:nh, :, None]
        a_col = cols[:, nh:2 * nh, :, None]
        y_dn = _deltanet(proj_f32, pool_width // HEAD_DIM, proj_b16, 0, b_col, a_col,
                         dn_conv_w[l], dn_a_log[l], dn_dt_bias[l], dn_onorm_w[l], b, s)

        bias_row = jnp.zeros((1, LANES), F32).at[0, 2 * nh:3 * nh].set(fox_f_bias[l])
        cum = _cumf(small, bias_row).transpose(0, 2, 1)[:, 2 * nh:3 * nh]
        y_fox = _fox(proj_b16, width // HEAD_DIM, cum[:, :, :, None], cum[:, :, None, :], b, s)

        merged = _merge(y_pool.reshape(t, -1), y_dn.reshape(t, -1), y_fox.reshape(t, -1),
                        w_branch_pool[l].astype(BF16), w_branch_dn[l].astype(BF16),
                        w_branch_fox[l].astype(BF16), gates)
        x = _out_proj(merged.reshape(b, s, d), w_out[l].astype(BF16), x, g_m)

        h2, h2_t = _norm_mod(x, norm_ffn_w[l], sc_f, sh_f, transposed_copy=True)
        sel = _peer_select(h2.reshape(t, d), peer_w_q[l].astype(BF16),
                           peer_sub_keys[l].reshape(2 * PEER_HEADS, N_SUBKEYS, -1).astype(BF16))
        y_t = _peer_dense(peer_u[l].astype(BF16), h2_t, peer_v[l].T.astype(BF16), *sel)
        x = _peer_residual(x, y_t, g_f)

    return _final_norm(x, final_norm_w)
```

```python
import functools

import jax
import jax.numpy as jnp
from jax import lax
from jax.experimental import pallas as pl
from jax.experimental.pallas import tpu as pltpu

F32 = jnp.float32
BF16 = jnp.bfloat16
EPS = 1e-6
MIB = 1024 * 1024

SUBLANES = 8
LANES = 128

CHUNK = 64
CONV_K = 4
POOL_WINDOWS = (2, 4, 8, 16)
POOL_HALO = 16
N_HEADS = 8
HEAD_DIM = 128
PEER_HEADS = 8
N_SUBKEYS = 128
PEER_TOPK = 16
NEG_INF = float("-inf")


def _params(semantics, vmem_mib):
    return pltpu.CompilerParams(dimension_semantics=semantics,
                                vmem_limit_bytes=vmem_mib * MIB)


def _dot(a, b):
    return jnp.dot(a, b, preferred_element_type=F32)


def _dot_nt(a, b):
    return lax.dot_general(a, b, (((1,), (1,)), ((), ())), preferred_element_type=F32)


def _dot_tn(a, b):
    return lax.dot_general(a, b, (((0,), (0,)), ((), ())), preferred_element_type=F32)


def _dot01(m01, x):
    n = x.shape[1]
    hi = x.astype(BF16)
    r1 = x - hi.astype(F32)
    mid = r1.astype(BF16)
    lo = (r1 - mid.astype(F32)).astype(BF16)
    y = _dot(m01, jnp.concatenate([hi, mid, lo], axis=1))
    return (y[:, :n] + y[:, n:2 * n]) + y[:, 2 * n:]


def _sigmoid(x):
    return 1.0 / (1.0 + jnp.exp(-x))


def _softplus(x):
    return jnp.maximum(x, 0.0) + jnp.log1p(jnp.exp(-jnp.abs(x)))


def _ada_body(c_ref, w_ref, b_ref, o_ref):
    c = c_ref[...]
    cs = (c * _sigmoid(c)).astype(BF16)
    o_ref[0] = _dot(cs, w_ref[0].astype(BF16)) + b_ref[0]


def _ada(c8, ada_w, ada_b, tn=1024):
    depth, d, n = ada_w.shape
    return pl.pallas_call(
        _ada_body,
        grid=(depth, n // tn),
        in_specs=[pl.BlockSpec((SUBLANES, d), lambda l, j: (0, 0)),
                  pl.BlockSpec((1, d, tn), lambda l, j: (l, 0, j)),
                  pl.BlockSpec((1, 1, tn), lambda l, j: (l, 0, j))],
        out_specs=pl.BlockSpec((1, SUBLANES, tn), lambda l, j: (l, 0, j)),
        out_shape=jax.ShapeDtypeStruct((depth, SUBLANES, n), F32),
        compiler_params=_params(("arbitrary", "arbitrary"), 32),
        name="ada_mod",
    )(c8, ada_w, ada_b.reshape(depth, 1, n))


def _rms(x, w):
    ms = jnp.mean(x * x, axis=-1, keepdims=True)
    return x * lax.rsqrt(ms + EPS) * w


def _norm_mod_body(x_ref, w_ref, sc_ref, sh_ref, o_ref, *maybe_t_ref):
    y = _rms(x_ref[0], w_ref[...]) * (1.0 + sc_ref[0]) + sh_ref[0]
    o_ref[0] = y.astype(o_ref.dtype)
    if maybe_t_ref:
        maybe_t_ref[0][...] = y.T.astype(BF16)


def _norm_mod(x, w, sc, sh, *, transposed_copy=False, ts=512):
    b, s, d = x.shape
    ts = min(ts, s)
    ns = s // ts
    out_shape = [jax.ShapeDtypeStruct((b, s, d), BF16)]
    out_specs = [pl.BlockSpec((1, ts, d), lambda i, j: (i, j, 0))]
    if transposed_copy:
        out_shape.append(jax.ShapeDtypeStruct((d, b * s), BF16))
        out_specs.append(pl.BlockSpec((d, ts), lambda i, j: (0, i * ns + j)))
    res = pl.pallas_call(
        _norm_mod_body,
        grid=(b, ns),
        in_specs=[pl.BlockSpec((1, ts, d), lambda i, j: (i, j, 0)),
                  pl.BlockSpec((1, d), lambda i, j: (0, 0)),
                  pl.BlockSpec((1, 1, d), lambda i, j: (i, 0, 0)),
                  pl.BlockSpec((1, 1, d), lambda i, j: (i, 0, 0))],
        out_specs=out_specs,
        out_shape=out_shape,
        compiler_params=_params(("arbitrary", "arbitrary"), 40),
        name="norm_mod",
    )(x, w.reshape(1, d), sc, sh)
    return res if transposed_copy else res[0]


def _final_norm_body(x_ref, w_ref, o_ref):
    o_ref[0] = _rms(x_ref[0], w_ref[...])


def _final_norm(x, w, ts=512):
    b, s, d = x.shape
    ts = min(ts, s)
    return pl.pallas_call(
        _final_norm_body,
        grid=(b, s // ts),
        in_specs=[pl.BlockSpec((1, ts, d), lambda i, j: (i, j, 0)),
                  pl.BlockSpec((1, d), lambda i, j: (0, 0))],
        out_specs=pl.BlockSpec((1, ts, d), lambda i, j: (i, j, 0)),
        out_shape=jax.ShapeDtypeStruct((b, s, d), F32),
        compiler_params=_params(("arbitrary", "arbitrary"), 32),
        name="final_norm",
    )(x, w.reshape(1, d))


def _mm_body(a_ref, w_ref, o_ref, *, sigmoid):
    acc = _dot(a_ref[...], w_ref[...])
    if sigmoid:
        acc = _sigmoid(acc)
    o_ref[...] = acc.astype(o_ref.dtype)


def _matmul(a, w, out_dtype, *, sigmoid=False, tm=1024, tn=1024):
    m, k = a.shape
    n = w.shape[1]
    tm, tn = min(tm, m), min(tn, n)
    return pl.pallas_call(
        functools.partial(_mm_body, sigmoid=sigmoid),
        grid=(m // tm, n // tn),
        in_specs=[pl.BlockSpec((tm, k), lambda i, j: (i, 0)),
                  pl.BlockSpec((k, tn), lambda i, j: (0, j))],
        out_specs=pl.BlockSpec((tm, tn), lambda i, j: (i, j)),
        out_shape=jax.ShapeDtypeStruct((m, n), out_dtype),
        compiler_params=_params(("arbitrary", "arbitrary"), 44),
        name="matmul",
    )(a, w)


def _pool_body(p_ref, w_ref, scale_ref, o_ref, ext_ref, *, ts, group):
    j = pl.program_id(1)

    @pl.when(j == 0)
    def _():
        ext_ref[0:POOL_HALO, :] = jnp.zeros((POOL_HALO, ext_ref.shape[1]), F32)

    ext_ref[POOL_HALO:POOL_HALO + ts, :] = p_ref[0]
    pos = (j * ts + lax.broadcasted_iota(jnp.int32, (ts, 1), 0) + 1).astype(F32)
    for g, win in enumerate(POOL_WINDOWS):
        cols = slice(g * group, (g + 1) * group)
        cur = ext_ref[POOL_HALO:POOL_HALO + ts, cols]
        acc = cur
        for back in range(1, win):
            acc = acc + ext_ref[POOL_HALO - back:POOL_HALO - back + ts, cols]
        pooled = acc / jnp.minimum(pos, float(win)) - cur
        y = _dot(pooled.astype(BF16), w_ref[g])
        o_ref[0, :, cols] = (y * scale_ref[:, cols]).astype(o_ref.dtype)
    ext_ref[0:POOL_HALO, :] = ext_ref[ts:ts + POOL_HALO, :]


def _pool(proj_f32, pool_w, pool_scale, b, s, ts=512):
    n_pool, group, _ = pool_w.shape
    width = n_pool * group
    ts = min(ts, s)
    return pl.pallas_call(
        functools.partial(_pool_body, ts=ts, group=group),
        grid=(b, s // ts),
        in_specs=[pl.BlockSpec((1, ts, width), lambda i, j: (i, j, 0)),
                  pl.BlockSpec((n_pool, group, group), lambda i, j: (0, 0, 0)),
                  pl.BlockSpec((1, width), lambda i, j: (0, 0))],
        out_specs=pl.BlockSpec((1, ts, width), lambda i, j: (i, j, 0)),
        out_shape=jax.ShapeDtypeStruct((b, s, width), BF16),
        scratch_shapes=[pltpu.VMEM((ts + POOL_HALO, width), F32)],
        compiler_params=_params(("arbitrary", "arbitrary"), 32),
        name="pool_mixer",
    )(proj_f32, pool_w, pool_scale.reshape(1, width))


def _cumf_body(x_ref, bias_ref, o_ref, carry_ref, *, ts):
    @pl.when(pl.program_id(1) == 0)
    def _():
        carry_ref[...] = jnp.zeros_like(carry_ref)

    z = x_ref[0] + bias_ref[...]
    log_f = jnp.minimum(z, 0.0) - jnp.log1p(jnp.exp(-jnp.abs(z)))
    row = lax.broadcasted_iota(jnp.int32, (ts, ts), 0)
    col = lax.broadcasted_iota(jnp.int32, (ts, ts), 1)
    tril = jnp.where(row >= col, 1.0, 0.0).astype(BF16)
    cs = _dot01(tril, log_f) + carry_ref[...]
    o_ref[0] = cs
    carry_ref[...] = cs[ts - 1:ts, :]


def _cumf(small, bias_row, ts=512):
    b, s, n = small.shape
    ts = min(ts, s)
    return pl.pallas_call(
        functools.partial(_cumf_body, ts=ts),
        grid=(b, s // ts),
        in_specs=[pl.BlockSpec((1, ts, n), lambda i, j: (i, j, 0)),
                  pl.BlockSpec((1, n), lambda i, j: (0, 0))],
        out_specs=pl.BlockSpec((1, ts, n), lambda i, j: (i, j, 0)),
        out_shape=jax.ShapeDtypeStruct((b, s, n), F32),
        scratch_shapes=[pltpu.VMEM((1, n), F32)],
        compiler_params=_params(("arbitrary", "arbitrary"), 32),
        name="cum_forget",
    )(small, bias_row)


DN_HEADS_PER_STEP = 2
PAIR = 2 * CHUNK


def _dn_pairs_body(q_ref, k_ref, v_ref, wq_ref, wk_ref, wv_ref, b_ref, a_ref, alog_ref, dtb_ref,
                   onw_ref, gate_ref, o_ref, state_ref, eq_ref, ek_ref, ev_ref, *, ts):
    halo = SUBLANES
    hg = DN_HEADS_PER_STEP
    n_pairs = ts // PAIR

    @pl.when(pl.program_id(2) == 0)
    def _():
        state_ref[...] = jnp.zeros_like(state_ref)
        for e_ref in (eq_ref, ek_ref, ev_ref):
            e_ref[0:halo, :] = jnp.zeros((halo, hg * HEAD_DIM), F32)

    def conv_silu(x_ref, e_ref, w_ref):
        e_ref[halo:halo + ts, :] = x_ref[0]
        w = w_ref[...]
        y = e_ref[halo:halo + ts, :] * w[CONV_K - 1:CONV_K, :]
        for back in range(1, CONV_K):
            tap = CONV_K - 1 - back
            y = y + e_ref[halo - back:halo - back + ts, :] * w[tap:tap + 1, :]
        e_ref[0:halo, :] = e_ref[ts:ts + halo, :]
        return y * _sigmoid(y)

    def l2n(x):
        return x * lax.rsqrt(jnp.sum(x * x, axis=-1, keepdims=True) + EPS)

    qc = conv_silu(q_ref, eq_ref, wq_ref)
    kc = conv_silu(k_ref, ek_ref, wk_ref)
    vc = conv_silu(v_ref, ev_ref, wv_ref)

    ri = lax.broadcasted_iota(jnp.int32, (PAIR, PAIR), 0)
    ci = lax.broadcasted_iota(jnp.int32, (PAIR, PAIR), 1)
    same = (ri >= CHUNK) == (ci >= CHUNK)
    incl = jnp.logical_and(same, ri >= ci)
    strict = jnp.logical_and(same, ri > ci)
    eye = jnp.where(ri == ci, 1.0, 0.0)
    tril01 = jnp.where(incl, 1.0, 0.0).astype(BF16)
    first = lax.broadcasted_iota(jnp.int32, (PAIR, 1), 0) < CHUNK

    qs, ks, vs, betas, gbs = [], [], [], [], []
    for h in range(hg):
        cols = slice(h * HEAD_DIM, (h + 1) * HEAD_DIM)
        q_h = l2n(qc[:, cols]) * (HEAD_DIM ** -0.5)
        k_h = l2n(kc[:, cols])
        v_h = vc[:, cols]
        beta_h = _sigmoid(b_ref[0, h])
        g_h = -jnp.exp(alog_ref[h]) * _softplus(a_ref[0, h] + dtb_ref[h])
        for p in range(n_pairs):
            rows = slice(p * PAIR, (p + 1) * PAIR)
            qs.append(q_h[rows])
            ks.append(k_h[rows])
            vs.append(v_h[rows])
            betas.append(beta_h[rows])
            gbs.append(jnp.broadcast_to(g_h[rows], (PAIR, HEAD_DIM)))

    def bf(x):
        return x.astype(BF16)

    gcum = [_dot01(tril01, gb) for gb in gbs]
    decay = [jnp.exp(jnp.where(incl, gc - gc.T, NEG_INF)) for gc in gcum]
    kb = [k * beta for k, beta in zip(ks, betas)]
    vb = [v * beta for v, beta in zip(vs, betas)]
    qk = [_dot_nt(bf(jnp.concatenate([q, kbi], axis=0)), bf(k)) for q, kbi, k in zip(qs, kb, ks)]
    a_intra = [m[:PAIR] * d for m, d in zip(qk, decay)]
    x = [-jnp.where(strict, m[PAIR:] * d, 0.0) for m, d in zip(qk, decay)]
    t_mat = [eye + xi for xi in x]
    pw = [_dot(bf(xi), bf(xi)) for xi in x]
    for _ in range(4):
        tp = [_dot(bf(jnp.concatenate([t, p], axis=0)), bf(p)) for t, p in zip(t_mat, pw)]
        t_mat = [t + m[:PAIR] for t, m in zip(t_mat, tp)]
        pw = [m[PAIR:] for m in tp]
    t_mat = [t + _dot(bf(t), bf(p)) for t, p in zip(t_mat, pw)]
    eg = [jnp.exp(gc) for gc in gcum]
    uw = [_dot(bf(t), bf(jnp.concatenate([vbi, kbi * e], axis=1)))
          for t, vbi, kbi, e in zip(t_mat, vb, kb, eg)]
    aw = [_dot(bf(a), bf(m)) for a, m in zip(a_intra, uw)]
    q_eff = [q * e - m[:, HEAD_DIM:] for q, e, m in zip(qs, eg, aw)]
    g_last = [jnp.where(first, gc[CHUNK - 1:CHUNK], gc[PAIR - 1:PAIR]) for gc in gcum]
    k_dec = [k * jnp.exp(gl - gc) for k, gl, gc in zip(ks, g_last, gcum)]
    k_split = [jnp.concatenate([jnp.where(first, kd, 0.0), jnp.where(first, 0.0, kd)], axis=1)
               for kd in k_dec]
    kw = [_dot_tn(bf(kd2), bf(m)) for kd2, m in zip(k_split, uw)]

    onw = onw_ref[...]
    states = [state_ref[h] for h in range(hg)]
    outs = [[] for _ in range(hg)]
    for p in range(n_pairs):
        for c in range(2):
            rows = slice(c * CHUNK, (c + 1) * CHUNK)
            krows = slice(c * HEAD_DIM, (c + 1) * HEAD_DIM)
            for h in range(hg):
                i = h * n_pairs + p
                lhs = jnp.concatenate([-kw[i][krows, HEAD_DIM:], q_eff[i][rows]], axis=0)
                res = _dot(bf(lhs), bf(states[h]))
                outs[h].append(res[HEAD_DIM:] + aw[i][rows, :HEAD_DIM])
                gamma = jnp.exp(gcum[i][(c + 1) * CHUNK - 1:(c + 1) * CHUNK])
                states[h] = states[h] * gamma + res[:HEAD_DIM] + kw[i][krows, :HEAD_DIM]
    for h in range(hg):
        cols = slice(h * HEAD_DIM, (h + 1) * HEAD_DIM)
        state_ref[h] = states[h]
        o = jnp.concatenate(outs[h], axis=0)
        o = o * lax.rsqrt(jnp.mean(o * o, axis=-1, keepdims=True) + EPS) * onw
        gate = gate_ref[0, :, cols].astype(F32)
        o_ref[0, :, cols] = (o * (gate * _sigmoid(gate))).astype(o_ref.dtype)


def _deltanet(proj_f32, qkv_col0, proj_bf16, gate_col0, b_col, a_col, conv_w, a_log, dt_bias,
              onorm_w, b, s, ts=512):
    ts = min(ts, s)
    hg = DN_HEADS_PER_STEP
    width = N_HEADS * HEAD_DIM
    wb = hg * HEAD_DIM
    n_groups = N_HEADS // hg

    def col_spec(col0):
        return pl.BlockSpec((1, ts, wb), lambda i, h, j: (i, j, col0 // wb + h))

    def conv_spec(col0):
        return pl.BlockSpec((CONV_K, wb), lambda i, h, j: (0, col0 // wb + h))

    column = pl.BlockSpec((1, hg, ts, 1), lambda i, h, j: (i, h, j, 0))
    scalar = pl.BlockSpec((hg, 1, 1), lambda i, h, j: (h, 0, 0))
    return pl.pallas_call(
        functools.partial(_dn_pairs_body, ts=ts),
        grid=(b, n_groups, s // ts),
        in_specs=[col_spec(qkv_col0), col_spec(qkv_col0 + width), col_spec(qkv_col0 + 2 * width),
                  conv_spec(0), conv_spec(width), conv_spec(2 * width),
                  column, column, scalar, scalar,
                  pl.BlockSpec((1, HEAD_DIM), lambda i, h, j: (0, 0)),
                  col_spec(gate_col0)],
        out_specs=pl.BlockSpec((1, ts, wb), lambda i, h, j: (i, j, h)),
        out_shape=jax.ShapeDtypeStruct((b, s, width), BF16),
        scratch_shapes=[pltpu.VMEM((hg, HEAD_DIM, HEAD_DIM), F32)] +
                       [pltpu.VMEM((ts + SUBLANES, wb), F32)] * 3,
        compiler_params=_params(("arbitrary", "arbitrary", "arbitrary"), 32),
        name="deltanet",
    )(proj_f32, proj_f32, proj_f32, conv_w, conv_w, conv_w, b_col, a_col,
      a_log.reshape(N_HEADS, 1, 1), dt_bias.reshape(N_HEADS, 1, 1),
      onorm_w.reshape(1, HEAD_DIM), proj_bf16)


def _fox_body(q_ref, k_ref, v_ref, cq_ref, ck_ref, o_ref, *, tq):
    qi = pl.program_id(2)
    q = q_ref[0]
    cq = cq_ref[0, 0]
    scale = HEAD_DIM ** -0.5
    row = qi * tq + lax.broadcasted_iota(jnp.int32, (tq, tq), 0)
    col0 = lax.broadcasted_iota(jnp.int32, (tq, tq), 1)

    def step(ki, carry):
        m, l, acc = carry
        start = pl.multiple_of(ki * tq, tq)
        k = k_ref[0, pl.ds(start, tq), :]
        v = v_ref[0, pl.ds(start, tq), :]
        s = _dot_nt(q, k) * scale + cq - ck_ref[0, 0, :, pl.ds(start, tq)]
        s = jnp.where(col0 + ki * tq <= row, s, NEG_INF)
        m_new = jnp.maximum(m, jnp.max(s, axis=-1, keepdims=True))
        p = jnp.exp(s - m_new)
        alpha = jnp.exp(m - m_new)
        l = alpha * l + jnp.sum(p, axis=-1, keepdims=True)
        acc = alpha * acc + _dot(p.astype(BF16), v)
        return m_new, l, acc

    init = (jnp.full((tq, 1), NEG_INF, F32), jnp.zeros((tq, 1), F32),
            jnp.zeros((tq, HEAD_DIM), F32))
    _, l, acc = lax.fori_loop(0, qi + 1, step, init)
    o_ref[0] = (acc / l).astype(o_ref.dtype)


def _fox(proj_bf16, qkv_col0, cum_q, cum_k, b, s, tq=512):
    tq = min(tq, s)
    width = N_HEADS * HEAD_DIM
    return pl.pallas_call(
        functools.partial(_fox_body, tq=tq),
        grid=(b, N_HEADS, s // tq),
        in_specs=[pl.BlockSpec((1, tq, HEAD_DIM), lambda i, h, j: (i, j, qkv_col0 + h)),
                  pl.BlockSpec((1, s, HEAD_DIM), lambda i, h, j: (i, 0, qkv_col0 + N_HEADS + h)),
                  pl.BlockSpec((1, s, HEAD_DIM), lambda i, h, j: (i, 0, qkv_col0 + 2 * N_HEADS + h)),
                  pl.BlockSpec((1, 1, tq, 1), lambda i, h, j: (i, h, j, 0)),
                  pl.BlockSpec((1, 1, 1, s), lambda i, h, j: (i, h, 0, 0))],
        out_specs=pl.BlockSpec((1, tq, HEAD_DIM), lambda i, h, j: (i, j, h)),
        out_shape=jax.ShapeDtypeStruct((b, s, width), BF16),
        compiler_params=_params(("arbitrary", "arbitrary", "arbitrary"), 32),
        name="fox_attention",
    )(proj_bf16, proj_bf16, proj_bf16, cum_q, cum_k)


def _merge_body(yp_ref, yd_ref, yf_ref, wp_ref, wd_ref, wf_ref, g0_ref, g1_ref, g2_ref, o_ref):
    acc = g0_ref[...].astype(F32) * _dot(yp_ref[...], wp_ref[...])
    acc = acc + g1_ref[...].astype(F32) * _dot(yd_ref[...], wd_ref[...])
    acc = acc + g2_ref[...].astype(F32) * _dot(yf_ref[...], wf_ref[...])
    o_ref[...] = acc.astype(o_ref.dtype)


def _merge(y_pool, y_dn, y_fox, w_pool, w_dn, w_fox, gates, tm=512, tn=1024):
    m, k = y_pool.shape
    n = w_pool.shape[1]
    tm = min(tm, m)
    nb = n // tn
    y_spec = pl.BlockSpec((tm, k), lambda i, j: (i, 0))
    w_spec = pl.BlockSpec((k, tn), lambda i, j: (0, j))

    def gate_spec(branch):
        return pl.BlockSpec((tm, tn), lambda i, j: (i, branch * nb + j))

    return pl.pallas_call(
        _merge_body,
        grid=(m // tm, nb),
        in_specs=[y_spec, y_spec, y_spec, w_spec, w_spec, w_spec,
                  gate_spec(0), gate_spec(1), gate_spec(2)],
        out_specs=pl.BlockSpec((tm, tn), lambda i, j: (i, j)),
        out_shape=jax.ShapeDtypeStruct((m, n), BF16),
        compiler_params=_params(("arbitrary", "arbitrary"), 44),
        name="branch_merge",
    )(y_pool, y_dn, y_fox, w_pool, w_dn, w_fox, gates, gates, gates)


def _out_proj_body(a_ref, w_ref, x_ref, g_ref, o_ref):
    o_ref[0] = x_ref[0] + g_ref[0] * _dot(a_ref[0], w_ref[...])


def _out_proj(merged, w_out, x, gate, tm=512, tn=1024):
    b, s, d = x.shape
    k = merged.shape[-1]
    tm = min(tm, s)
    return pl.pallas_call(
        _out_proj_body,
        grid=(b, s // tm, d // tn),
        in_specs=[pl.BlockSpec((1, tm, k), lambda i, j, n: (i, j, 0)),
                  pl.BlockSpec((k, tn), lambda i, j, n: (0, n)),
                  pl.BlockSpec((1, tm, tn), lambda i, j, n: (i, j, n)),
                  pl.BlockSpec((1, 1, tn), lambda i, j, n: (i, 0, n))],
        out_specs=pl.BlockSpec((1, tm, tn), lambda i, j, n: (i, j, n)),
        out_shape=jax.ShapeDtypeStruct((b, s, d), F32),
        compiler_params=_params(("arbitrary", "arbitrary", "arbitrary"), 40),
        name="out_proj_residual",
    )(merged, w_out, x, gate)


_CANDIDATES = [(a, b) for a in range(PEER_TOPK) for b in range(PEER_TOPK)
               if (a + 1) * (b + 1) <= PEER_TOPK]
_CROSS_PAIRS = [(c, d) for c in _CANDIDATES for d in _CANDIDATES if c[0] < d[0] and c[1] > d[1]]
_SELECT_INTERLEAVE = 2


def _peer_select_body(h_ref, wq_ref, sk_ref, e1_ref, cnt_ref, e2_ref, rank2_ref,
                      sc_ref, rk_ref, tp_ref, cn_ref, zi_ref, *, tt):
    n_lt = tt // LANES
    n_half = PEER_HEADS * n_lt
    q = _dot(h_ref[...], wq_ref[...])
    for hp in range(2 * PEER_HEADS):
        head, half = divmod(hp, 2)
        q_hp = q[:, hp * N_SUBKEYS:(hp + 1) * N_SUBKEYS].astype(BF16)
        s_t = _dot_nt(sk_ref[hp], q_hp)
        for lt in range(n_lt):
            sc_ref[half * n_half + head * n_lt + lt] = s_t[:, lt * LANES:(lt + 1) * LANES]

    key_id = lax.broadcasted_iota(jnp.int32, (N_SUBKEYS, LANES), 0).astype(F32)

    def half_topk(it, carry):
        base = it * _SELECT_INTERLEAVE
        vs = [sc_ref[base + k] for k in range(_SELECT_INTERLEAVE)]
        ranks = [jnp.full((N_SUBKEYS, LANES), float(PEER_TOPK), F32)] * _SELECT_INTERLEAVE
        for r in range(PEER_TOPK):
            ms = [jnp.max(v, axis=0, keepdims=True) for v in vs]
            firsts = [jnp.min(jnp.where(v == m, key_id, float(N_SUBKEYS)), axis=0, keepdims=True)
                      for v, m in zip(vs, ms)]
            hits = [key_id == first for first in firsts]
            ranks = [jnp.where(hit, float(r), rank) for hit, rank in zip(hits, ranks)]
            vs = [jnp.where(hit, NEG_INF, v) for hit, v in zip(hits, vs)]
            for k in range(_SELECT_INTERLEAVE):
                tp_ref[r, pl.ds(base + k, 1), :] = ms[k]
        for k in range(_SELECT_INTERLEAVE):
            rk_ref[base + k] = ranks[k]
        return carry

    lax.fori_loop(0, 2 * n_half // _SELECT_INTERLEAVE, half_topk, 0)

    def pair_topk(g, carry):
        b1 = pl.multiple_of(g * SUBLANES, SUBLANES)
        b2 = b1 + n_half
        t1 = [tp_ref[a, pl.ds(b1, SUBLANES), :] for a in range(PEER_TOPK)]
        t2 = [tp_ref[b, pl.ds(b2, SUBLANES), :] for b in range(PEER_TOPK)]
        cand = {(a, b): t1[a] + t2[b] for a, b in _CANDIDATES}
        beaten = {(a, b): jnp.full((SUBLANES, LANES), float((a + 1) * (b + 1) - 1), F32)
                  for a, b in _CANDIDATES}
        for c, d in _CROSS_PAIRS:
            c_first = cand[c] >= cand[d]
            beaten[d] = beaten[d] + jnp.where(c_first, 1.0, 0.0)
            beaten[c] = beaten[c] + jnp.where(c_first, 0.0, 1.0)
        z = jnp.zeros((SUBLANES, LANES), F32)
        cnts = [jnp.zeros((SUBLANES, LANES), F32) for _ in range(PEER_TOPK)]
        for a, b in _CANDIDATES:
            sel = jnp.where(beaten[(a, b)] < float(PEER_TOPK), 1.0, 0.0)
            cnts[a] = cnts[a] + sel
            z = z + sel * jnp.exp(cand[(a, b)] - cand[(0, 0)])
        for a in range(PEER_TOPK):
            cn_ref[a, pl.ds(b1, SUBLANES), :] = cnts[a]
        zi_ref[pl.ds(b1, SUBLANES), :] = 1.0 / z
        return carry

    lax.fori_loop(0, n_half // SUBLANES, pair_topk, 0)

    def emit(i1, carry):
        i2 = i1 + n_half
        head = i1 // n_lt
        lt = i1 % n_lt
        rank1 = rk_ref[i1]
        cnt = jnp.zeros((N_SUBKEYS, LANES), F32)
        for a in range(PEER_TOPK):
            cnt = jnp.where(rank1 == float(a), cn_ref[a, pl.ds(i1, 1), :], cnt)
        e1 = jnp.exp(sc_ref[i1] - tp_ref[0, pl.ds(i1, 1), :]) * zi_ref[pl.ds(i1, 1), :]
        e1_ref[head, lt] = e1
        cnt_ref[head, lt] = cnt
        e2 = jnp.exp(sc_ref[i2] - tp_ref[0, pl.ds(i2, 1), :])
        e2_ref[head, lt] = pltpu.bitcast(e2.astype(BF16), jnp.uint32)
        rank2_ref[head, lt] = pltpu.bitcast(rk_ref[i2].astype(BF16), jnp.uint32)
        return carry

    lax.fori_loop(0, n_half, emit, 0)


def _peer_select(h, w_q, sub_keys, tt=512):
    t, d = h.shape
    tt = min(tt, t)
    n_lt = tt // LANES
    n_prob = 2 * PEER_HEADS * n_lt
    dims = (PEER_HEADS, t // LANES, N_SUBKEYS, LANES)
    packed_dims = (PEER_HEADS, t // LANES, N_SUBKEYS // 2, LANES)
    spec = pl.BlockSpec((PEER_HEADS, n_lt, N_SUBKEYS, LANES), lambda i: (0, i, 0, 0))
    packed_spec = pl.BlockSpec((PEER_HEADS, n_lt, N_SUBKEYS // 2, LANES), lambda i: (0, i, 0, 0))
    return pl.pallas_call(
        functools.partial(_peer_select_body, tt=tt),
        grid=(t // tt,),
        in_specs=[pl.BlockSpec((tt, d), lambda i: (i, 0)),
                  pl.BlockSpec(w_q.shape, lambda i: (0, 0)),
                  pl.BlockSpec(sub_keys.shape, lambda i: (0, 0, 0))],
        out_specs=[spec, spec, packed_spec, packed_spec],
        out_shape=[jax.ShapeDtypeStruct(dims, F32), jax.ShapeDtypeStruct(dims, F32),
                   jax.ShapeDtypeStruct(packed_dims, jnp.uint32),
                   jax.ShapeDtypeStruct(packed_dims, jnp.uint32)],
        scratch_shapes=[pltpu.VMEM((n_prob, N_SUBKEYS, LANES), F32),
                        pltpu.VMEM((n_prob, N_SUBKEYS, LANES), F32),
                        pltpu.VMEM((PEER_TOPK, n_prob, LANES), F32),
                        pltpu.VMEM((PEER_TOPK, n_prob // 2, LANES), F32),
                        pltpu.VMEM((n_prob // 2, LANES), F32)],
        compiler_params=_params(("arbitrary",), 48),
        name="peer_select",
    )(h, w_q, sub_keys)


def _peer_dense_body(u0_ref, ht0_ref, u_ref, ht_ref, vt_ref, e1_ref, cnt_ref, e2_ref, rank2_ref,
                     o_ref, act_a_ref, act_b_ref, coef_ref, *, tt, te):
    n_lt = tt // LANES
    n_i = te // N_SUBKEYS
    bf16_rows = 2 * SUBLANES
    zero = jnp.zeros((), BF16)
    step = pl.program_id(0) * pl.num_programs(1) + pl.program_id(1)

    def row_bf16(ref, head, lt, ii):
        row = jnp.broadcast_to(ref[head, lt, ii:ii + 1, :], (bf16_rows, LANES)).astype(BF16)
        return jnp.tile(row, (N_SUBKEYS // bf16_rows, 1))

    @pl.when(step == 0)
    def _():
        act_a_ref[...] = _dot(u0_ref[...], ht0_ref[...])

    @pl.when(pl.program_id(1) == 0)
    def _():
        o_ref[...] = jnp.zeros_like(o_ref)

    def work(act_ref, next_act_ref):
        next_act_ref[...] = _dot(u_ref[...], ht_ref[...])
        for ii in range(n_i):
            rows = slice(ii * N_SUBKEYS, (ii + 1) * N_SUBKEYS)
            for lt in range(n_lt):
                lanes = slice(lt * LANES, (lt + 1) * LANES)
                w = jnp.zeros((N_SUBKEYS, LANES), BF16)
                for head in range(PEER_HEADS):
                    cnt = row_bf16(cnt_ref, head, lt, ii)
                    e1 = row_bf16(e1_ref, head, lt, ii)
                    rank2 = pltpu.bitcast(rank2_ref[head, lt], BF16)
                    e2 = pltpu.bitcast(e2_ref[head, lt], BF16)
                    w = w + jnp.where(rank2 < cnt, e2, zero) * e1
                a = act_ref[rows, lanes]
                gelu = 0.5 * a * (1.0 + lax.erf(a * (2.0 ** -0.5)))
                coef_ref[rows, lanes] = gelu.astype(BF16) * w
        o_ref[...] += _dot(vt_ref[...], coef_ref[...])

    @pl.when(lax.rem(step, 2) == 0)
    def _():
        work(act_a_ref, act_b_ref)

    @pl.when(lax.rem(step, 2) == 1)
    def _():
        work(act_b_ref, act_a_ref)


def _peer_dense(u, h_t, v_t, e1, cnt, e2, rank2, tt=512, te=1024):
    n_exp, d = u.shape
    t = h_t.shape[1]
    tt = min(tt, t)
    n_lt = tt // LANES
    n_i = te // N_SUBKEYS
    ni, nj = t // tt, n_exp // te
    full = pl.BlockSpec((PEER_HEADS, n_lt, N_SUBKEYS // 2, LANES), lambda i, j: (0, i, 0, 0))
    rows = pl.BlockSpec((PEER_HEADS, n_lt, n_i, LANES), lambda i, j: (0, i, j, 0))
    once = pl.Buffered(1)

    def next_token_tile(i, j):
        return jnp.minimum(i + (j + 1) // nj, ni - 1)

    return pl.pallas_call(
        functools.partial(_peer_dense_body, tt=tt, te=te),
        grid=(ni, nj),
        in_specs=[pl.BlockSpec((te, d), lambda i, j: (0, 0), pipeline_mode=once),
                  pl.BlockSpec((d, tt), lambda i, j: (0, 0), pipeline_mode=once),
                  pl.BlockSpec((te, d), lambda i, j: ((j + 1) % nj, 0)),
                  pl.BlockSpec((d, tt), lambda i, j: (0, next_token_tile(i, j))),
                  pl.BlockSpec((d, te), lambda i, j: (0, j)),
                  rows, rows, full, full],
        out_specs=pl.BlockSpec((d, tt), lambda i, j: (0, i)),
        out_shape=jax.ShapeDtypeStruct((d, t), F32),
        scratch_shapes=[pltpu.VMEM((te, tt), F32), pltpu.VMEM((te, tt), F32),
                        pltpu.VMEM((te, tt), BF16)],
        compiler_params=_params(("arbitrary", "arbitrary"), 50),
        name="peer_dense",
    )(u, h_t, u, h_t, v_t, e1, cnt, e2, rank2)


def _peer_residual_body(x_ref, yt_ref, g_ref, o_ref):
    o_ref[0] = x_ref[0] + g_ref[0] * yt_ref[...].T


def _peer_residual(x, y_t, gate, ts=512):
    b, s, d = x.shape
    ts = min(ts, s)
    ns = s // ts
    return pl.pallas_call(
        _peer_residual_body,
        grid=(b, ns),
        in_specs=[pl.BlockSpec((1, ts, d), lambda i, j: (i, j, 0)),
                  pl.BlockSpec((d, ts), lambda i, j: (0, i * ns + j)),
                  pl.BlockSpec((1, 1, d), lambda i, j: (i, 0, 0))],
        out_specs=pl.BlockSpec((1, ts, d), lambda i, j: (i, j, 0)),
        out_shape=jax.ShapeDtypeStruct((b, s, d), F32),
        compiler_params=_params(("arbitrary", "arbitrary"), 40),
        name="peer_residual",
    )(x, y_t, gate)


def kernel(x, c, ada_w, ada_b, norm_mix_w, w_in, pool_w, pool_scale, dn_conv_w, dn_a_log,
           dn_dt_bias, dn_onorm_w, fox_f_bias, w_branch_pool, w_branch_dn, w_branch_fox,
           w_out, norm_ffn_w, peer_w_q, peer_sub_keys, peer_u, peer_v, final_norm_w):
    b, s, d = x.shape
    t = b * s
    depth = ada_w.shape[0]
    pool_width = pool_w.shape[1] * pool_w.shape[2]
    width = N_HEADS * HEAD_DIM
    nh = N_HEADS

    c8 = jnp.zeros((SUBLANES, d), F32).at[:b].set(c)
    mod = _ada(c8, ada_w, ada_b)[:, :b].reshape(depth, b, 6, 1, d)

    o_pool = 0
    o_dnqkv = o_pool + pool_width
    o_dnb = o_dnqkv + 3 * width
    o_dna = o_dnb + nh
    o_dng = o_dna + nh
    o_foxqkv = o_dng + width
    o_foxf = o_foxqkv + 3 * width
    o_gates = o_foxf + nh

    for l in range(depth):
        sh_m, sc_m, g_m, sh_f, sc_f, g_f = (mod[l, :, i] for i in range(6))
        wl = w_in[l]
        w_f32 = wl[:, o_pool:o_dnb].astype(BF16)
        w_b16 = jnp.concatenate([wl[:, o_dng:o_foxqkv], wl[:, o_foxqkv:o_foxf]], axis=1).astype(BF16)
        w_gate = wl[:, o_gates:].astype(BF16)
        w_small = jnp.concatenate(
            [wl[:, o_dnb:o_dng], wl[:, o_foxf:o_gates],
             jnp.zeros((d, LANES - 3 * nh), F32)], axis=1).astype(BF16)

        h = _norm_mod(x, norm_mix_w[l], sc_m, sh_m).reshape(t, d)
        proj_f32 = _matmul(h, w_f32, F32).reshape(b, s, -1)
        proj_b16 = _matmul(h, w_b16, BF16).reshape(b, s, -1)
        gates = _matmul(h, w_gate, BF16, sigmoid=True)
        small = _matmul(h, w_small, F32).reshape(b, s, LANES)

        y_pool = _pool(proj_f32, pool_w[l].astype(BF16), pool_scale[l], b, s)

        cols = small.transpose(0, 2, 1)
        b_col = cols[:, 0:nh, :, None]
        a_col = cols[:, nh:2 * nh, :, None]
        y_dn = _deltanet(proj_f32, pool_width, proj_b16, 0, b_col, a_col,
                         dn_conv_w[l], dn_a_log[l], dn_dt_bias[l], dn_onorm_w[l], b, s)

        bias_row = jnp.zeros((1, LANES), F32).at[0, 2 * nh:3 * nh].set(fox_f_bias[l])
        cum = _cumf(small, bias_row).transpose(0, 2, 1)[:, 2 * nh:3 * nh]
        y_fox = _fox(proj_b16, width // HEAD_DIM, cum[:, :, :, None], cum[:, :, None, :], b, s)

        merged = _merge(y_pool.reshape(t, -1), y_dn.reshape(t, -1), y_fox.reshape(t, -1),
                        w_branch_pool[l].astype(BF16), w_branch_dn[l].astype(BF16),
                        w_branch_fox[l].astype(BF16), gates)
        x = _out_proj(merged.reshape(b, s, d), w_out[l].astype(BF16), x, g_m)

        h2, h2_t = _norm_mod(x, norm_ffn_w[l], sc_f, sh_f, transposed_copy=True)
        sel = _peer_select(h2.reshape(t, d), peer_w_q[l].astype(BF16),
                           peer_sub_keys[l].reshape(2 * PEER_HEADS, N_SUBKEYS, -1).astype(BF16))
        y_t = _peer_dense(peer_u[l].astype(BF16), h2_t, peer_v[l].T.astype(BF16), *sel)
        x = _peer_residual(x, y_t, g_f)

    return _final_norm(x, final_norm_w)
```

```python
import functools

import jax
import jax.numpy as jnp
from jax import lax
from jax.experimental import pallas as pl
from jax.experimental.pallas import tpu as pltpu

F32 = jnp.float32
BF16 = jnp.bfloat16
EPS = 1e-6
MIB = 1024 * 1024

SUBLANES = 8
LANES = 128

CHUNK = 64
CONV_K = 4
POOL_WINDOWS = (2, 4, 8, 16)
POOL_HALO = 16
N_HEADS = 8
HEAD_DIM = 128
PEER_HEADS = 8
N_SUBKEYS = 128
PEER_TOPK = 16
NEG_INF = float("-inf")


def _params(semantics, vmem_mib):
    return pltpu.CompilerParams(dimension_semantics=semantics,
                                vmem_limit_bytes=vmem_mib * MIB)


def _dot(a, b):
    return jnp.dot(a, b, preferred_element_type=F32)


def _dot_nt(a, b):
    return lax.dot_general(a, b, (((1,), (1,)), ((), ())), preferred_element_type=F32)


def _dot_tn(a, b):
    return lax.dot_general(a, b, (((0,), (0,)), ((), ())), preferred_element_type=F32)


def _dot01(m01, x):
    n = x.shape[1]
    hi = x.astype(BF16)
    r1 = x - hi.astype(F32)
    mid = r1.astype(BF16)
    lo = (r1 - mid.astype(F32)).astype(BF16)
    y = _dot(m01, jnp.concatenate([hi, mid, lo], axis=1))
    return (y[:, :n] + y[:, n:2 * n]) + y[:, 2 * n:]


def _sigmoid(x):
    return 1.0 / (1.0 + jnp.exp(-x))


def _softplus(x):
    return jnp.maximum(x, 0.0) + jnp.log1p(jnp.exp(-jnp.abs(x)))


def _ada_body(c_ref, w_ref, b_ref, o_ref):
    c = c_ref[...]
    cs = (c * _sigmoid(c)).astype(BF16)
    o_ref[0] = _dot(cs, w_ref[0].astype(BF16)) + b_ref[0]


def _ada(c8, ada_w, ada_b, tn=1024):
    depth, d, n = ada_w.shape
    return pl.pallas_call(
        _ada_body,
        grid=(depth, n // tn),
        in_specs=[pl.BlockSpec((SUBLANES, d), lambda l, j: (0, 0)),
                  pl.BlockSpec((1, d, tn), lambda l, j: (l, 0, j)),
                  pl.BlockSpec((1, 1, tn), lambda l, j: (l, 0, j))],
        out_specs=pl.BlockSpec((1, SUBLANES, tn), lambda l, j: (l, 0, j)),
        out_shape=jax.ShapeDtypeStruct((depth, SUBLANES, n), F32),
        compiler_params=_params(("arbitrary", "arbitrary"), 32),
        name="ada_mod",
    )(c8, ada_w, ada_b.reshape(depth, 1, n))


def _rms(x, w):
    ms = jnp.mean(x * x, axis=-1, keepdims=True)
    return x * lax.rsqrt(ms + EPS) * w


def _norm_mod_body(x_ref, w_ref, sc_ref, sh_ref, o_ref):
    y = _rms(x_ref[0], w_ref[...]) * (1.0 + sc_ref[0]) + sh_ref[0]
    o_ref[0] = y.astype(o_ref.dtype)


def _norm_mod(x, w, sc, sh, ts=512):
    b, s, d = x.shape
    ts = min(ts, s)
    return pl.pallas_call(
        _norm_mod_body,
        grid=(b, s // ts),
        in_specs=[pl.BlockSpec((1, ts, d), lambda i, j: (i, j, 0)),
                  pl.BlockSpec((1, d), lambda i, j: (0, 0)),
                  pl.BlockSpec((1, 1, d), lambda i, j: (i, 0, 0)),
                  pl.BlockSpec((1, 1, d), lambda i, j: (i, 0, 0))],
        out_specs=pl.BlockSpec((1, ts, d), lambda i, j: (i, j, 0)),
        out_shape=jax.ShapeDtypeStruct((b, s, d), BF16),
        compiler_params=_params(("arbitrary", "arbitrary"), 40),
        name="norm_mod",
    )(x, w.reshape(1, d), sc, sh)


def _mm_body(a_ref, w_ref, o_ref, *, sigmoid):
    acc = _dot(a_ref[...], w_ref[...])
    if sigmoid:
        acc = _sigmoid(acc)
    o_ref[...] = acc.astype(o_ref.dtype)


def _matmul(a, w, out_dtype, *, sigmoid=False, tm=1024, tn=1024):
    m, k = a.shape
    n = w.shape[1]
    tm, tn = min(tm, m), min(tn, n)
    return pl.pallas_call(
        functools.partial(_mm_body, sigmoid=sigmoid),
        grid=(m // tm, n // tn),
        in_specs=[pl.BlockSpec((tm, k), lambda i, j: (i, 0)),
                  pl.BlockSpec((k, tn), lambda i, j: (0, j))],
        out_specs=pl.BlockSpec((tm, tn), lambda i, j: (i, j)),
        out_shape=jax.ShapeDtypeStruct((m, n), out_dtype),
        compiler_params=_params(("arbitrary", "arbitrary"), 44),
        name="matmul",
    )(a, w)


def _pool_body(p_ref, w_ref, scale_ref, o_ref, ext_ref, *, ts, group):
    j = pl.program_id(1)

    @pl.when(j == 0)
    def _():
        ext_ref[0:POOL_HALO, :] = jnp.zeros((POOL_HALO, ext_ref.shape[1]), F32)

    ext_ref[POOL_HALO:POOL_HALO + ts, :] = p_ref[0]
    pos = (j * ts + lax.broadcasted_iota(jnp.int32, (ts, 1), 0) + 1).astype(F32)
    for g, win in enumerate(POOL_WINDOWS):
        cols = slice(g * group, (g + 1) * group)
        cur = ext_ref[POOL_HALO:POOL_HALO + ts, cols]
        acc = cur
        for back in range(1, win):
            acc = acc + ext_ref[POOL_HALO - back:POOL_HALO - back + ts, cols]
        pooled = acc / jnp.minimum(pos, float(win)) - cur
        y = _dot(pooled.astype(BF16), w_ref[g])
        o_ref[0, :, cols] = (y * scale_ref[:, cols]).astype(o_ref.dtype)
    ext_ref[0:POOL_HALO, :] = ext_ref[ts:ts + POOL_HALO, :]


def _pool(proj_f32, pool_w, pool_scale, b, s, ts=512):
    n_pool, group, _ = pool_w.shape
    width = n_pool * group
    ts = min(ts, s)
    return pl.pallas_call(
        functools.partial(_pool_body, ts=ts, group=group),
        grid=(b, s // ts),
        in_specs=[pl.BlockSpec((1, ts, width), lambda i, j: (i, j, 0)),
                  pl.BlockSpec((n_pool, group, group), lambda i, j: (0, 0, 0)),
                  pl.BlockSpec((1, width), lambda i, j: (0, 0))],
        out_specs=pl.BlockSpec((1, ts, width), lambda i, j: (i, j, 0)),
        out_shape=jax.ShapeDtypeStruct((b, s, width), BF16),
        scratch_shapes=[pltpu.VMEM((ts + POOL_HALO, width), F32)],
        compiler_params=_params(("arbitrary", "arbitrary"), 32),
        name="pool_mixer",
    )(proj_f32, pool_w, pool_scale.reshape(1, width))


def _cumf_body(x_ref, bias_ref, o_ref, carry_ref, *, ts):
    @pl.when(pl.program_id(1) == 0)
    def _():
        carry_ref[...] = jnp.zeros_like(carry_ref)

    z = x_ref[0] + bias_ref[...]
    log_f = jnp.minimum(z, 0.0) - jnp.log1p(jnp.exp(-jnp.abs(z)))
    row = lax.broadcasted_iota(jnp.int32, (ts, ts), 0)
    col = lax.broadcasted_iota(jnp.int32, (ts, ts), 1)
    tril = jnp.where(row >= col, 1.0, 0.0).astype(BF16)
    cs = _dot01(tril, log_f) + carry_ref[...]
    o_ref[0] = cs
    carry_ref[...] = cs[ts - 1:ts, :]


def _cumf(small, bias_row, ts=512):
    b, s, n = small.shape
    ts = min(ts, s)
    return pl.pallas_call(
        functools.partial(_cumf_body, ts=ts),
        grid=(b, s // ts),
        in_specs=[pl.BlockSpec((1, ts, n), lambda i, j: (i, j, 0)),
                  pl.BlockSpec((1, n), lambda i, j: (0, 0))],
        out_specs=pl.BlockSpec((1, ts, n), lambda i, j: (i, j, 0)),
        out_shape=jax.ShapeDtypeStruct((b, s, n), F32),
        scratch_shapes=[pltpu.VMEM((1, n), F32)],
        compiler_params=_params(("arbitrary", "arbitrary"), 32),
        name="cum_forget",
    )(small, bias_row)


DN_HEADS_PER_STEP = 2
PAIR = 2 * CHUNK


def _dn_pairs_body(q_ref, k_ref, v_ref, wq_ref, wk_ref, wv_ref, b_ref, a_ref, alog_ref, dtb_ref,
                   onw_ref, gate_ref, o_ref, state_ref, eq_ref, ek_ref, ev_ref, *, ts):
    halo = SUBLANES
    hg = DN_HEADS_PER_STEP
    n_pairs = ts // PAIR

    @pl.when(pl.program_id(2) == 0)
    def _():
        state_ref[...] = jnp.zeros_like(state_ref)
        for e_ref in (eq_ref, ek_ref, ev_ref):
            e_ref[0:halo, :] = jnp.zeros((halo, hg * HEAD_DIM), F32)

    def conv_silu(x_ref, e_ref, w_ref):
        e_ref[halo:halo + ts, :] = x_ref[0]
        w = w_ref[...]
        y = e_ref[halo:halo + ts, :] * w[CONV_K - 1:CONV_K, :]
        for back in range(1, CONV_K):
            tap = CONV_K - 1 - back
            y = y + e_ref[halo - back:halo - back + ts, :] * w[tap:tap + 1, :]
        e_ref[0:halo, :] = e_ref[ts:ts + halo, :]
        return y * _sigmoid(y)

    def l2n(x):
        return x * lax.rsqrt(jnp.sum(x * x, axis=-1, keepdims=True) + EPS)

    qc = conv_silu(q_ref, eq_ref, wq_ref)
    kc = conv_silu(k_ref, ek_ref, wk_ref)
    vc = conv_silu(v_ref, ev_ref, wv_ref)

    ri = lax.broadcasted_iota(jnp.int32, (PAIR, PAIR), 0)
    ci = lax.broadcasted_iota(jnp.int32, (PAIR, PAIR), 1)
    same = (ri >= CHUNK) == (ci >= CHUNK)
    incl = jnp.logical_and(same, ri >= ci)
    strict = jnp.logical_and(same, ri > ci)
    eye = jnp.where(ri == ci, 1.0, 0.0)
    tril01 = jnp.where(incl, 1.0, 0.0).astype(BF16)
    first = lax.broadcasted_iota(jnp.int32, (PAIR, 1), 0) < CHUNK

    qs, ks, vs, betas, gbs = [], [], [], [], []
    for h in range(hg):
        cols = slice(h * HEAD_DIM, (h + 1) * HEAD_DIM)
        q_h = l2n(qc[:, cols]) * (HEAD_DIM ** -0.5)
        k_h = l2n(kc[:, cols])
        v_h = vc[:, cols]
        beta_h = _sigmoid(b_ref[0, h])
        g_h = -jnp.exp(alog_ref[h]) * _softplus(a_ref[0, h] + dtb_ref[h])
        for p in range(n_pairs):
            rows = slice(p * PAIR, (p + 1) * PAIR)
            qs.append(q_h[rows])
            ks.append(k_h[rows])
            vs.append(v_h[rows])
            betas.append(beta_h[rows])
            gbs.append(jnp.broadcast_to(g_h[rows], (PAIR, HEAD_DIM)))

    def bf(x):
        return x.astype(BF16)

    gcum = [_dot01(tril01, gb) for gb in gbs]
    decay = [jnp.exp(jnp.where(incl, gc - gc.T, NEG_INF)) for gc in gcum]
    kb = [k * beta for k, beta in zip(ks, betas)]
    vb = [v * beta for v, beta in zip(vs, betas)]
    qk = [_dot_nt(bf(jnp.concatenate([q, kbi], axis=0)), bf(k)) for q, kbi, k in zip(qs, kb, ks)]
    a_intra = [m[:PAIR] * d for m, d in zip(qk, decay)]
    x = [-jnp.where(strict, m[PAIR:] * d, 0.0) for m, d in zip(qk, decay)]
    t_mat = [eye + xi for xi in x]
    pw = [_dot(bf(xi), bf(xi)) for xi in x]
    for _ in range(4):
        tp = [_dot(bf(jnp.concatenate([t, p], axis=0)), bf(p)) for t, p in zip(t_mat, pw)]
        t_mat = [t + m[:PAIR] for t, m in zip(t_mat, tp)]
        pw = [m[PAIR:] for m in tp]
    t_mat = [t + _dot(bf(t), bf(p)) for t, p in zip(t_mat, pw)]
    eg = [jnp.exp(gc) for gc in gcum]
    uw = [_dot(bf(t), bf(jnp.concatenate([vbi, kbi * e], axis=1)))
          for t, vbi, kbi, e in zip(t_mat, vb, kb, eg)]
    aw = [_dot(bf(a), bf(m)) for a, m in zip(a_intra, uw)]
    q_eff = [q * e - m[:, HEAD_DIM:] for q, e, m in zip(qs, eg, aw)]
    g_last = [jnp.where(first, gc[CHUNK - 1:CHUNK], gc[PAIR - 1:PAIR]) for gc in gcum]
    k_dec = [k * jnp.exp(gl - gc) for k, gl, gc in zip(ks, g_last, gcum)]
    k_split = [jnp.concatenate([jnp.where(first, kd, 0.0), jnp.where(first, 0.0, kd)], axis=1)
               for kd in k_dec]
    kw = [_dot_tn(bf(kd2), bf(m)) for kd2, m in zip(k_split, uw)]

    onw = onw_ref[...]
    states = [state_ref[h] for h in range(hg)]
    outs = [[] for _ in range(hg)]
    for p in range(n_pairs):
        for c in range(2):
            rows = slice(c * CHUNK, (c + 1) * CHUNK)
            krows = slice(c * HEAD_DIM, (c + 1) * HEAD_DIM)
            for h in range(hg):
                i = h * n_pairs + p
                lhs = jnp.concatenate([-kw[i][krows, HEAD_DIM:], q_eff[i][rows]], axis=0)
                res = _dot(bf(lhs), bf(states[h]))
                outs[h].append(res[HEAD_DIM:] + aw[i][rows, :HEAD_DIM])
                gamma = jnp.exp(gcum[i][(c + 1) * CHUNK - 1:(c + 1) * CHUNK])
                states[h] = states[h] * gamma + res[:HEAD_DIM] + kw[i][krows, :HEAD_DIM]
    for h in range(hg):
        cols = slice(h * HEAD_DIM, (h + 1) * HEAD_DIM)
        state_ref[h] = states[h]
        o = jnp.concatenate(outs[h], axis=0)
        o = o * lax.rsqrt(jnp.mean(o * o, axis=-1, keepdims=True) + EPS) * onw
        gate = gate_ref[0, :, cols].astype(F32)
        o_ref[0, :, cols] = (o * (gate * _sigmoid(gate))).astype(o_ref.dtype)


def _deltanet(proj_f32, qkv_col0, proj_bf16, gate_col0, b_col, a_col, conv_w, a_log, dt_bias,
              onorm_w, b, s, ts=512):
    ts = min(ts, s)
    hg = DN_HEADS_PER_STEP
    width = N_HEADS * HEAD_DIM
    wb = hg * HEAD_DIM
    n_groups = N_HEADS // hg

    def col_spec(col0):
        return pl.BlockSpec((1, ts, wb), lambda i, h, j: (i, j, col0 // wb + h))

    def conv_spec(col0):
        return pl.BlockSpec((CONV_K, wb), lambda i, h, j: (0, col0 // wb + h))

    column = pl.BlockSpec((1, hg, ts, 1), lambda i, h, j: (i, h, j, 0))
    scalar = pl.BlockSpec((hg, 1, 1), lambda i, h, j: (h, 0, 0))
    return pl.pallas_call(
        functools.partial(_dn_pairs_body, ts=ts),
        grid=(b, n_groups, s // ts),
        in_specs=[col_spec(qkv_col0), col_spec(qkv_col0 + width), col_spec(qkv_col0 + 2 * width),
                  conv_spec(0), conv_spec(width), conv_spec(2 * width),
                  column, column, scalar, scalar,
                  pl.BlockSpec((1, HEAD_DIM), lambda i, h, j: (0, 0)),
                  col_spec(gate_col0)],
        out_specs=pl.BlockSpec((1, ts, wb), lambda i, h, j: (i, j, h)),
        out_shape=jax.ShapeDtypeStruct((b, s, width), BF16),
        scratch_shapes=[pltpu.VMEM((hg, HEAD_DIM, HEAD_DIM), F32)] +
                       [pltpu.VMEM((ts + SUBLANES, wb), F32)] * 3,
        compiler_params=_params(("arbitrary", "arbitrary", "arbitrary"), 32),
        name="deltanet",
    )(proj_f32, proj_f32, proj_f32, conv_w, conv_w, conv_w, b_col, a_col,
      a_log.reshape(N_HEADS, 1, 1), dt_bias.reshape(N_HEADS, 1, 1),
      onorm_w.reshape(1, HEAD_DIM), proj_bf16)


LOG2E = 1.4426950408889634
FOX_AUG = 2 * HEAD_DIM


def _fox_body(q_ref, k_ref, v_ref, cq_ref, ck_ref, o_ref, kaug_ref, vt_ref, *, tq, s_len):
    qi = pl.program_id(2)
    hg = FOX_HEADS_PER_STEP
    heads = range(hg)

    @pl.when(qi == 0)
    def _():
        lane = lax.broadcasted_iota(jnp.int32, (s_len, HEAD_DIM), 1)
        for h in heads:
            cols = slice(h * HEAD_DIM, (h + 1) * HEAD_DIM)
            c = ck_ref[0, h] * LOG2E
            hi = c.astype(BF16)
            r1 = c - hi.astype(F32)
            mid = r1.astype(BF16)
            lo = (r1 - mid.astype(F32)).astype(BF16)
            pieces = jnp.where(lane == 0, hi.astype(F32),
                               jnp.where(lane == 1, mid.astype(F32),
                                         jnp.where(lane == 2, lo.astype(F32), 0.0)))
            kaug_ref[h, :, :HEAD_DIM] = k_ref[0, :, cols]
            kaug_ref[h, :, HEAD_DIM:] = pieces.astype(BF16)
            for t in range(s_len // tq):
                rows = slice(t * tq, (t + 1) * tq)
                vt_ref[h, :, rows] = v_ref[0, rows, cols].astype(F32).T.astype(BF16)

    lane = lax.broadcasted_iota(jnp.int32, (tq, HEAD_DIM), 1)
    minus_one = jnp.where(lane < 3, -1.0, 0.0).astype(BF16)
    q_aug = [jnp.concatenate(
        [(q_ref[0, :, h * HEAD_DIM:(h + 1) * HEAD_DIM].astype(F32)
          * (HEAD_DIM ** -0.5 * LOG2E)).astype(BF16), minus_one], axis=1) for h in heads]
    cq = [cq_ref[0, h] * LOG2E for h in heads]

    def step(ki, carry, diagonal):
        ms, ls, accs = carry
        start = pl.multiple_of(ki * tq, tq)
        s_t = [_dot_nt(kaug_ref[h, pl.ds(start, tq), :], q_aug[h]) for h in heads]
        if diagonal:
            key = lax.broadcasted_iota(jnp.int32, (tq, tq), 0)
            qry = lax.broadcasted_iota(jnp.int32, (tq, tq), 1)
            s_t = [jnp.where(key <= qry, s, NEG_INF) for s in s_t]
        m_new = [jnp.maximum(ms[h], jnp.max(s_t[h], axis=0, keepdims=True) + cq[h]) for h in heads]
        p_t = [jnp.exp2(s_t[h] - (m_new[h] - cq[h])) for h in heads]
        alpha = [jnp.exp2(ms[h] - m_new[h]) for h in heads]
        l_new = [alpha[h] * ls[h] + jnp.sum(p_t[h], axis=0, keepdims=True) for h in heads]
        acc_new = [alpha[h] * accs[h] + _dot(vt_ref[h, :, pl.ds(start, tq)], p_t[h].astype(BF16))
                   for h in heads]
        return tuple(m_new), tuple(l_new), tuple(acc_new)

    init = (tuple(jnp.full((1, tq), NEG_INF, F32) for _ in heads),
            tuple(jnp.zeros((1, tq), F32) for _ in heads),
            tuple(jnp.zeros((HEAD_DIM, tq), F32) for _ in heads))
    carry = lax.fori_loop(0, qi, functools.partial(step, diagonal=False), init)
    _, ls, accs = step(qi, carry, diagonal=True)
    for h in heads:
        o_ref[0, :, h * HEAD_DIM:(h + 1) * HEAD_DIM] = (accs[h] / ls[h]).T.astype(o_ref.dtype)


FOX_HEADS_PER_STEP = 2


def _fox(proj_bf16, qkv_col0, cum_row, cum_col, b, s, tq=512):
    tq = min(tq, s)
    hg = FOX_HEADS_PER_STEP
    width = N_HEADS * HEAD_DIM
    wb = hg * HEAD_DIM
    return pl.pallas_call(
        functools.partial(_fox_body, tq=tq, s_len=s),
        grid=(b, N_HEADS // hg, s // tq),
        in_specs=[pl.BlockSpec((1, tq, wb), lambda i, h, j: (i, j, qkv_col0 // wb + h)),
                  pl.BlockSpec((1, s, wb), lambda i, h, j: (i, 0, (qkv_col0 + width) // wb + h)),
                  pl.BlockSpec((1, s, wb), lambda i, h, j: (i, 0, (qkv_col0 + 2 * width) // wb + h)),
                  pl.BlockSpec((1, hg, 1, tq), lambda i, h, j: (i, h, 0, j)),
                  pl.BlockSpec((1, hg, s, 1), lambda i, h, j: (i, h, 0, 0))],
        out_specs=pl.BlockSpec((1, tq, wb), lambda i, h, j: (i, j, h)),
        out_shape=jax.ShapeDtypeStruct((b, s, width), BF16),
        scratch_shapes=[pltpu.VMEM((hg, s, FOX_AUG), BF16), pltpu.VMEM((hg, HEAD_DIM, s), BF16)],
        compiler_params=_params(("arbitrary", "arbitrary", "arbitrary"), 40),
        name="fox_attention",
    )(proj_bf16, proj_bf16, proj_bf16, cum_row, cum_col)


def _merge_body(yp_ref, yd_ref, yf_ref, wp_ref, wd_ref, wf_ref, g0_ref, g1_ref, g2_ref, o_ref):
    acc = g0_ref[...].astype(F32) * _dot(yp_ref[...], wp_ref[...])
    acc = acc + g1_ref[...].astype(F32) * _dot(yd_ref[...], wd_ref[...])
    acc = acc + g2_ref[...].astype(F32) * _dot(yf_ref[...], wf_ref[...])
    o_ref[...] = acc.astype(o_ref.dtype)


def _merge(y_pool, y_dn, y_fox, w_pool, w_dn, w_fox, gates, tm=512, tn=1024):
    m, k = y_pool.shape
    n = w_pool.shape[1]
    tm = min(tm, m)
    nb = n // tn
    y_spec = pl.BlockSpec((tm, k), lambda i, j: (i, 0))
    w_spec = pl.BlockSpec((k, tn), lambda i, j: (0, j))

    def gate_spec(branch):
        return pl.BlockSpec((tm, tn), lambda i, j: (i, branch * nb + j))

    return pl.pallas_call(
        _merge_body,
        grid=(m // tm, nb),
        in_specs=[y_spec, y_spec, y_spec, w_spec, w_spec, w_spec,
                  gate_spec(0), gate_spec(1), gate_spec(2)],
        out_specs=pl.BlockSpec((tm, tn), lambda i, j: (i, j)),
        out_shape=jax.ShapeDtypeStruct((m, n), BF16),
        compiler_params=_params(("arbitrary", "arbitrary"), 44),
        name="branch_merge",
    )(y_pool, y_dn, y_fox, w_pool, w_dn, w_fox, gates, gates, gates)


def _out_proj_body(a_ref, w_ref, x_ref, g_ref, nw_ref, sc_ref, sh_ref, x_out_ref, h_ref, ht_ref):
    x = x_ref[0] + g_ref[0] * _dot(a_ref[0], w_ref[...])
    x_out_ref[0] = x
    y = _rms(x, nw_ref[...]) * (1.0 + sc_ref[0]) + sh_ref[0]
    h_ref[0] = y.astype(h_ref.dtype)
    ht_ref[...] = y.T.astype(ht_ref.dtype)


def _out_proj(merged, w_out, x, gate, norm_w, sc, sh, tm=512):
    b, s, d = x.shape
    k = merged.shape[-1]
    tm = min(tm, s)
    ns = s // tm
    tile = pl.BlockSpec((1, tm, d), lambda i, j: (i, j, 0))
    per_batch = pl.BlockSpec((1, 1, d), lambda i, j: (i, 0, 0))
    return pl.pallas_call(
        _out_proj_body,
        grid=(b, ns),
        in_specs=[pl.BlockSpec((1, tm, k), lambda i, j: (i, j, 0)),
                  pl.BlockSpec((k, d), lambda i, j: (0, 0), pipeline_mode=pl.Buffered(1)),
                  tile, per_batch,
                  pl.BlockSpec((1, d), lambda i, j: (0, 0)), per_batch, per_batch],
        out_specs=[tile, tile, pl.BlockSpec((d, tm), lambda i, j: (0, i * ns + j))],
        out_shape=[jax.ShapeDtypeStruct((b, s, d), F32), jax.ShapeDtypeStruct((b, s, d), BF16),
                   jax.ShapeDtypeStruct((d, b * s), BF16)],
        compiler_params=_params(("arbitrary", "arbitrary"), 48),
        name="out_proj_residual",
    )(merged, w_out, x, gate, norm_w.reshape(1, d), sc, sh)


_CANDIDATES = [(a, b) for a in range(PEER_TOPK) for b in range(PEER_TOPK)
               if (a + 1) * (b + 1) <= PEER_TOPK]
_CROSS_PAIRS = [(c, d) for c in _CANDIDATES for d in _CANDIDATES if c[0] < d[0] and c[1] > d[1]]
_SELECT_INTERLEAVE = 2


def _peer_select_body(h_ref, wq_ref, sk_ref, e1_ref, cnt_ref, e2_ref, rank2_ref,
                      sc_ref, rk_ref, tp_ref, cn_ref, zi_ref, *, tt):
    n_lt = tt // LANES
    n_half = PEER_HEADS * n_lt
    q = _dot(h_ref[...], wq_ref[...])
    for hp in range(2 * PEER_HEADS):
        head, half = divmod(hp, 2)
        q_hp = q[:, hp * N_SUBKEYS:(hp + 1) * N_SUBKEYS].astype(BF16)
        s_t = _dot_nt(sk_ref[hp], q_hp)
        for lt in range(n_lt):
            sc_ref[half * n_half + head * n_lt + lt] = s_t[:, lt * LANES:(lt + 1) * LANES]

    key_id = lax.broadcasted_iota(jnp.int32, (N_SUBKEYS, LANES), 0).astype(F32)

    def half_topk(it, carry):
        base = it * _SELECT_INTERLEAVE
        vs = [sc_ref[base + k] for k in range(_SELECT_INTERLEAVE)]
        ranks = [jnp.full((N_SUBKEYS, LANES), float(PEER_TOPK), F32)] * _SELECT_INTERLEAVE
        for r in range(PEER_TOPK):
            ms = [jnp.max(v, axis=0, keepdims=True) for v in vs]
            firsts = [jnp.min(jnp.where(v == m, key_id, float(N_SUBKEYS)), axis=0, keepdims=True)
                      for v, m in zip(vs, ms)]
            hits = [key_id == first for first in firsts]
            ranks = [jnp.where(hit, float(r), rank) for hit, rank in zip(hits, ranks)]
            vs = [jnp.where(hit, NEG_INF, v) for hit, v in zip(hits, vs)]
            for k in range(_SELECT_INTERLEAVE):
                tp_ref[r, pl.ds(base + k, 1), :] = ms[k]
        for k in range(_SELECT_INTERLEAVE):
            rk_ref[base + k] = ranks[k]
        return carry

    lax.fori_loop(0, 2 * n_half // _SELECT_INTERLEAVE, half_topk, 0)

    def pair_topk(g, carry):
        b1 = pl.multiple_of(g * SUBLANES, SUBLANES)
        b2 = b1 + n_half
        t1 = [tp_ref[a, pl.ds(b1, SUBLANES), :] for a in range(PEER_TOPK)]
        t2 = [tp_ref[b, pl.ds(b2, SUBLANES), :] for b in range(PEER_TOPK)]
        cand = {(a, b): t1[a] + t2[b] for a, b in _CANDIDATES}
        beaten = {(a, b): jnp.full((SUBLANES, LANES), float((a + 1) * (b + 1) - 1), F32)
                  for a, b in _CANDIDATES}
        for c, d in _CROSS_PAIRS:
            c_first = cand[c] >= cand[d]
            beaten[d] = beaten[d] + jnp.where(c_first, 1.0, 0.0)
            beaten[c] = beaten[c] + jnp.where(c_first, 0.0, 1.0)
        z = jnp.zeros((SUBLANES, LANES), F32)
        cnts = [jnp.zeros((SUBLANES, LANES), F32) for _ in range(PEER_TOPK)]
        for a, b in _CANDIDATES:
            sel = jnp.where(beaten[(a, b)] < float(PEER_TOPK), 1.0, 0.0)
            cnts[a] = cnts[a] + sel
            z = z + sel * jnp.exp(cand[(a, b)] - cand[(0, 0)])
        for a in range(PEER_TOPK):
            cn_ref[a, pl.ds(b1, SUBLANES), :] = cnts[a]
        zi_ref[pl.ds(b1, SUBLANES), :] = 1.0 / z
        return carry

    lax.fori_loop(0, n_half // SUBLANES, pair_topk, 0)

    def emit(i1, carry):
        i2 = i1 + n_half
        head = i1 // n_lt
        lt = i1 % n_lt
        rank1 = rk_ref[i1]
        cnt = jnp.zeros((N_SUBKEYS, LANES), F32)
        for a in range(PEER_TOPK):
            cnt = jnp.where(rank1 == float(a), cn_ref[a, pl.ds(i1, 1), :], cnt)
        e1 = jnp.exp(sc_ref[i1] - tp_ref[0, pl.ds(i1, 1), :]) * zi_ref[pl.ds(i1, 1), :]
        e1_ref[head, lt] = e1
        cnt_ref[head, lt] = cnt
        e2 = jnp.exp(sc_ref[i2] - tp_ref[0, pl.ds(i2, 1), :])
        e2_ref[head, lt] = pltpu.bitcast(e2.astype(BF16), jnp.uint32)
        rank2_ref[head, lt] = pltpu.bitcast(rk_ref[i2].astype(BF16), jnp.uint32)
        return carry

    lax.fori_loop(0, n_half, emit, 0)


def _peer_select(h, w_q, sub_keys, tt=512):
    t, d = h.shape
    tt = min(tt, t)
    n_lt = tt // LANES
    n_prob = 2 * PEER_HEADS * n_lt
    dims = (PEER_HEADS, t // LANES, N_SUBKEYS, LANES)
    packed_dims = (PEER_HEADS, t // LANES, N_SUBKEYS // 2, LANES)
    spec = pl.BlockSpec((PEER_HEADS, n_lt, N_SUBKEYS, LANES), lambda i: (0, i, 0, 0))
    packed_spec = pl.BlockSpec((PEER_HEADS, n_lt, N_SUBKEYS // 2, LANES), lambda i: (0, i, 0, 0))
    return pl.pallas_call(
        functools.partial(_peer_select_body, tt=tt),
        grid=(t // tt,),
        in_specs=[pl.BlockSpec((tt, d), lambda i: (i, 0)),
                  pl.BlockSpec(w_q.shape, lambda i: (0, 0)),
                  pl.BlockSpec(sub_keys.shape, lambda i: (0, 0, 0))],
        out_specs=[spec, spec, packed_spec, packed_spec],
        out_shape=[jax.ShapeDtypeStruct(dims, F32), jax.ShapeDtypeStruct(dims, F32),
                   jax.ShapeDtypeStruct(packed_dims, jnp.uint32),
                   jax.ShapeDtypeStruct(packed_dims, jnp.uint32)],
        scratch_shapes=[pltpu.VMEM((n_prob, N_SUBKEYS, LANES), F32),
                        pltpu.VMEM((n_prob, N_SUBKEYS, LANES), F32),
                        pltpu.VMEM((PEER_TOPK, n_prob, LANES), F32),
                        pltpu.VMEM((PEER_TOPK, n_prob // 2, LANES), F32),
                        pltpu.VMEM((n_prob // 2, LANES), F32)],
        compiler_params=_params(("arbitrary",), 48),
        name="peer_select",
    )(h, w_q, sub_keys)


def _peer_dense_body(u0_ref, ht0_ref, u_ref, ht_ref, vt_ref, e1_ref, cnt_ref, e2_ref, rank2_ref,
                     o_ref, act_a_ref, act_b_ref, coef_ref, *, tt, te):
    n_lt = tt // LANES
    n_i = te // N_SUBKEYS
    bf16_rows = 2 * SUBLANES
    zero = jnp.zeros((), BF16)
    step = pl.program_id(0) * pl.num_programs(1) + pl.program_id(1)

    def row_bf16(ref, head, lt, ii):
        row = jnp.broadcast_to(ref[head, lt, ii:ii + 1, :], (bf16_rows, LANES)).astype(BF16)
        return jnp.tile(row, (N_SUBKEYS // bf16_rows, 1))

    @pl.when(step == 0)
    def _():
        act_a_ref[...] = _dot(u0_ref[...], ht0_ref[...])

    @pl.when(pl.program_id(1) == 0)
    def _():
        o_ref[...] = jnp.zeros_like(o_ref)

    def work(act_ref, next_act_ref):
        next_act_ref[...] = _dot(u_ref[...], ht_ref[...])
        for ii in range(n_i):
            rows = slice(ii * N_SUBKEYS, (ii + 1) * N_SUBKEYS)
            for lt in range(n_lt):
                lanes = slice(lt * LANES, (lt + 1) * LANES)
                w = jnp.zeros((N_SUBKEYS, LANES), BF16)
                for head in range(PEER_HEADS):
                    cnt = row_bf16(cnt_ref, head, lt, ii)
                    e1 = row_bf16(e1_ref, head, lt, ii)
                    rank2 = pltpu.bitcast(rank2_ref[head, lt], BF16)
                    e2 = pltpu.bitcast(e2_ref[head, lt], BF16)
                    w = w + jnp.where(rank2 < cnt, e2, zero) * e1
                a = act_ref[rows, lanes]
                gelu = 0.5 * a * (1.0 + lax.erf(a * (2.0 ** -0.5)))
                coef_ref[rows, lanes] = gelu.astype(BF16) * w
        o_ref[...] += _dot(vt_ref[...], coef_ref[...])

    @pl.when(lax.rem(step, 2) == 0)
    def _():
        work(act_a_ref, act_b_ref)

    @pl.when(lax.rem(step, 2) == 1)
    def _():
        work(act_b_ref, act_a_ref)


def _peer_dense(u, h_t, v_t, e1, cnt, e2, rank2, tt=512, te=1024):
    n_exp, d = u.shape
    t = h_t.shape[1]
    tt = min(tt, t)
    n_lt = tt // LANES
    n_i = te // N_SUBKEYS
    ni, nj = t // tt, n_exp // te
    full = pl.BlockSpec((PEER_HEADS, n_lt, N_SUBKEYS // 2, LANES), lambda i, j: (0, i, 0, 0))
    rows = pl.BlockSpec((PEER_HEADS, n_lt, n_i, LANES), lambda i, j: (0, i, j, 0))
    once = pl.Buffered(1)

    def next_token_tile(i, j):
        return jnp.minimum(i + (j + 1) // nj, ni - 1)

    return pl.pallas_call(
        functools.partial(_peer_dense_body, tt=tt, te=te),
        grid=(ni, nj),
        in_specs=[pl.BlockSpec((te, d), lambda i, j: (0, 0), pipeline_mode=once),
                  pl.BlockSpec((d, tt), lambda i, j: (0, 0), pipeline_mode=once),
                  pl.BlockSpec((te, d), lambda i, j: ((j + 1) % nj, 0)),
                  pl.BlockSpec((d, tt), lambda i, j: (0, next_token_tile(i, j))),
                  pl.BlockSpec((d, te), lambda i, j: (0, j)),
                  rows, rows, full, full],
        out_specs=pl.BlockSpec((d, tt), lambda i, j: (0, i)),
        out_shape=jax.ShapeDtypeStruct((d, t), F32),
        scratch_shapes=[pltpu.VMEM((te, tt), F32), pltpu.VMEM((te, tt), F32),
                        pltpu.VMEM((te, tt), BF16)],
        compiler_params=_params(("arbitrary", "arbitrary"), 50),
        name="peer_dense",
    )(u, h_t, u, h_t, v_t, e1, cnt, e2, rank2)


def _peer_residual_body(x_ref, yt_ref, g_ref, w_ref, sc_ref, sh_ref, x_out_ref, h_ref):
    x = x_ref[0] + g_ref[0] * yt_ref[...].T
    x_out_ref[0] = x
    h_ref[0] = (_rms(x, w_ref[...]) * (1.0 + sc_ref[0]) + sh_ref[0]).astype(h_ref.dtype)


def _peer_residual_final_body(x_ref, yt_ref, g_ref, w_ref, o_ref):
    o_ref[0] = _rms(x_ref[0] + g_ref[0] * yt_ref[...].T, w_ref[...])


def _peer_residual(x, y_t, gate, norm_w, sc=None, sh=None, ts=512):
    b, s, d = x.shape
    ts = min(ts, s)
    ns = s // ts
    tile = pl.BlockSpec((1, ts, d), lambda i, j: (i, j, 0))
    per_batch = pl.BlockSpec((1, 1, d), lambda i, j: (i, 0, 0))
    in_specs = [tile, pl.BlockSpec((d, ts), lambda i, j: (0, i * ns + j)), per_batch,
                pl.BlockSpec((1, d), lambda i, j: (0, 0))]
    args = [x, y_t, gate, norm_w.reshape(1, d)]
    if sc is None:
        body, out_specs, out_shape = _peer_residual_final_body, tile, jax.ShapeDtypeStruct((b, s, d), F32)
    else:
        body = _peer_residual_body
        in_specs += [per_batch, per_batch]
        args += [sc, sh]
        out_specs = [tile, tile]
        out_shape = [jax.ShapeDtypeStruct((b, s, d), F32), jax.ShapeDtypeStruct((b, s, d), BF16)]
    return pl.pallas_call(
        body,
        grid=(b, ns),
        in_specs=in_specs,
        out_specs=out_specs,
        out_shape=out_shape,
        compiler_params=_params(("arbitrary", "arbitrary"), 44),
        name="peer_residual",
    )(*args)


def kernel(x, c, ada_w, ada_b, norm_mix_w, w_in, pool_w, pool_scale, dn_conv_w, dn_a_log,
           dn_dt_bias, dn_onorm_w, fox_f_bias, w_branch_pool, w_branch_dn, w_branch_fox,
           w_out, norm_ffn_w, peer_w_q, peer_sub_keys, peer_u, peer_v, final_norm_w):
    b, s, d = x.shape
    t = b * s
    depth = ada_w.shape[0]
    pool_width = pool_w.shape[1] * pool_w.shape[2]
    width = N_HEADS * HEAD_DIM
    nh = N_HEADS

    c8 = jnp.zeros((SUBLANES, d), F32).at[:b].set(c)
    mod = _ada(c8, ada_w, ada_b)[:, :b].reshape(depth, b, 6, 1, d)

    o_pool = 0
    o_dnqkv = o_pool + pool_width
    o_dnb = o_dnqkv + 3 * width
    o_dna = o_dnb + nh
    o_dng = o_dna + nh
    o_foxqkv = o_dng + width
    o_foxf = o_foxqkv + 3 * width
    o_gates = o_foxf + nh

    for l in range(depth):
        sh_m, sc_m, g_m, sh_f, sc_f, g_f = (mod[l, :, i] for i in range(6))
        wl = w_in[l]
        w_f32 = wl[:, o_pool:o_dnb].astype(BF16)
        w_b16 = jnp.concatenate([wl[:, o_dng:o_foxqkv], wl[:, o_foxqkv:o_foxf]], axis=1).astype(BF16)
        w_gate = wl[:, o_gates:].astype(BF16)
        w_small = jnp.concatenate(
            [wl[:, o_dnb:o_dng], wl[:, o_foxf:o_gates],
             jnp.zeros((d, LANES - 3 * nh), F32)], axis=1).astype(BF16)

        if l == 0:
            h = _norm_mod(x, norm_mix_w[l], sc_m, sh_m)
        h = h.reshape(t, d)
        proj_f32 = _matmul(h, w_f32, F32).reshape(b, s, -1)
        proj_b16 = _matmul(h, w_b16, BF16).reshape(b, s, -1)
        gates = _matmul(h, w_gate, BF16, sigmoid=True)
        small = _matmul(h, w_small, F32).reshape(b, s, LANES)

        y_pool = _pool(proj_f32, pool_w[l].astype(BF16), pool_scale[l], b, s)

        cols = small.transpose(0, 2, 1)
        b_col = cols[:, 0:nh, :, None]
        a_col = cols[:, nh:2 * nh, :, None]
        y_dn = _deltanet(proj_f32, pool_width, proj_b16, 0, b_col, a_col,
                         dn_conv_w[l], dn_a_log[l], dn_dt_bias[l], dn_onorm_w[l], b, s)

        bias_row = jnp.zeros((1, LANES), F32).at[0, 2 * nh:3 * nh].set(fox_f_bias[l])
        cum = _cumf(small, bias_row).transpose(0, 2, 1)[:, 2 * nh:3 * nh]
        y_fox = _fox(proj_b16, width, cum[:, :, None, :], cum[:, :, :, None], b, s)

        merged = _merge(y_pool.reshape(t, -1), y_dn.reshape(t, -1), y_fox.reshape(t, -1),
                        w_branch_pool[l].astype(BF16), w_branch_dn[l].astype(BF16),
                        w_branch_fox[l].astype(BF16), gates)
        x, h2, h2_t = _out_proj(merged.reshape(b, s, d), w_out[l].astype(BF16), x, g_m,
                                norm_ffn_w[l], sc_f, sh_f)
        sel = _peer_select(h2.reshape(t, d), peer_w_q[l].astype(BF16),
                           peer_sub_keys[l].reshape(2 * PEER_HEADS, N_SUBKEYS, -1).astype(BF16))
        y_t = _peer_dense(peer_u[l].astype(BF16), h2_t, peer_v[l].T.astype(BF16), *sel)
        if l + 1 < depth:
            x, h = _peer_residual(x, y_t, g_f, norm_mix_w[l + 1], mod[l + 1, :, 1], mod[l + 1, :, 0])
        else:
            out = _peer_residual(x, y_t, g_f, final_norm_w)
    return out
```

```python
import functools

import jax
import jax.numpy as jnp
from jax import lax
from jax.experimental import pallas as pl
from jax.experimental.pallas import tpu as pltpu

F32 = jnp.float32
BF16 = jnp.bfloat16
EPS = 1e-6
MIB = 1024 * 1024

SUBLANES = 8
LANES = 128

CHUNK = 64
CONV_K = 4
POOL_WINDOWS = (2, 4, 8, 16)
POOL_HALO = 16
N_HEADS = 8
HEAD_DIM = 128
PEER_HEADS = 8
N_SUBKEYS = 128
PEER_TOPK = 16
NEG_INF = float("-inf")


def _params(semantics, vmem_mib):
    return pltpu.CompilerParams(dimension_semantics=semantics,
                                vmem_limit_bytes=vmem_mib * MIB)


def _dot(a, b):
    return jnp.dot(a, b, preferred_element_type=F32)


def _dot_nt(a, b):
    return lax.dot_general(a, b, (((1,), (1,)), ((), ())), preferred_element_type=F32)


def _dot_tn(a, b):
    return lax.dot_general(a, b, (((0,), (0,)), ((), ())), preferred_element_type=F32)


def _split3(x):
    hi = x.astype(BF16)
    r1 = x - hi.astype(F32)
    mid = r1.astype(BF16)
    lo = (r1 - mid.astype(F32)).astype(BF16)
    return hi, mid, lo


def _dot01(m01, x):
    n = x.shape[1]
    y = _dot(m01, jnp.concatenate(_split3(x), axis=1))
    return (y[:, :n] + y[:, n:2 * n]) + y[:, 2 * n:]


def _pick_columns(x, pick01):
    m = x.shape[0]
    y = _dot(jnp.concatenate(_split3(x), axis=0), pick01)
    return (y[:m] + y[m:2 * m]) + y[2 * m:]


def _sigmoid(x):
    return 1.0 / (1.0 + jnp.exp(-x))


def _softplus(x):
    return jnp.maximum(x, 0.0) + jnp.log1p(jnp.exp(-jnp.abs(x)))


def _ada_body(c_ref, w_ref, b_ref, o_ref):
    c = c_ref[...]
    cs = (c * _sigmoid(c)).astype(BF16)
    o_ref[0] = _dot(cs, w_ref[0].astype(BF16)) + b_ref[0]


def _ada(c8, ada_w, ada_b, tn=1024):
    depth, d, n = ada_w.shape
    return pl.pallas_call(
        _ada_body,
        grid=(depth, n // tn),
        in_specs=[pl.BlockSpec((SUBLANES, d), lambda l, j: (0, 0)),
                  pl.BlockSpec((1, d, tn), lambda l, j: (l, 0, j)),
                  pl.BlockSpec((1, 1, tn), lambda l, j: (l, 0, j))],
        out_specs=pl.BlockSpec((1, SUBLANES, tn), lambda l, j: (l, 0, j)),
        out_shape=jax.ShapeDtypeStruct((depth, SUBLANES, n), F32),
        compiler_params=_params(("arbitrary", "arbitrary"), 32),
        name="ada_mod",
    )(c8, ada_w, ada_b.reshape(depth, 1, n))


def _rms(x, w):
    ms = jnp.mean(x * x, axis=-1, keepdims=True)
    return x * lax.rsqrt(ms + EPS) * w


def _norm_mod_body(x_ref, w_ref, sc_ref, sh_ref, o_ref):
    y = _rms(x_ref[0], w_ref[...]) * (1.0 + sc_ref[0]) + sh_ref[0]
    o_ref[0] = y.astype(o_ref.dtype)


def _norm_mod(x, w, sc, sh, ts=512):
    b, s, d = x.shape
    ts = min(ts, s)
    return pl.pallas_call(
        _norm_mod_body,
        grid=(b, s // ts),
        in_specs=[pl.BlockSpec((1, ts, d), lambda i, j: (i, j, 0)),
                  pl.BlockSpec((1, d), lambda i, j: (0, 0)),
                  pl.BlockSpec((1, 1, d), lambda i, j: (i, 0, 0)),
                  pl.BlockSpec((1, 1, d), lambda i, j: (i, 0, 0))],
        out_specs=pl.BlockSpec((1, ts, d), lambda i, j: (i, j, 0)),
        out_shape=jax.ShapeDtypeStruct((b, s, d), BF16),
        compiler_params=_params(("arbitrary", "arbitrary"), 40),
        name="norm_mod",
    )(x, w.reshape(1, d), sc, sh)


def _mm_body(a_ref, w_ref, o_ref, *, sigmoid):
    acc = _dot(a_ref[...], w_ref[...])
    if sigmoid:
        acc = _sigmoid(acc)
    o_ref[...] = acc.astype(o_ref.dtype)


def _matmul(a, w, out_dtype, *, sigmoid=False, tm=1024, tn=1024):
    m, k = a.shape
    n = w.shape[1]
    tm, tn = min(tm, m), min(tn, n)
    return pl.pallas_call(
        functools.partial(_mm_body, sigmoid=sigmoid),
        grid=(m // tm, n // tn),
        in_specs=[pl.BlockSpec((tm, k), lambda i, j: (i, 0)),
                  pl.BlockSpec((k, tn), lambda i, j: (0, j))],
        out_specs=pl.BlockSpec((tm, tn), lambda i, j: (i, j)),
        out_shape=jax.ShapeDtypeStruct((m, n), out_dtype),
        compiler_params=_params(("arbitrary", "arbitrary"), 44),
        name="matmul",
    )(a, w)


def _pool_body(p_ref, w_ref, scale_ref, o_ref, ext_ref, *, ts, group):
    j = pl.program_id(1)

    @pl.when(j == 0)
    def _():
        ext_ref[0:POOL_HALO, :] = jnp.zeros((POOL_HALO, ext_ref.shape[1]), F32)

    ext_ref[POOL_HALO:POOL_HALO + ts, :] = p_ref[0]
    pos = (j * ts + lax.broadcasted_iota(jnp.int32, (ts, 1), 0) + 1).astype(F32)
    for g, win in enumerate(POOL_WINDOWS):
        cols = slice(g * group, (g + 1) * group)
        cur = ext_ref[POOL_HALO:POOL_HALO + ts, cols]
        acc = cur
        for back in range(1, win):
            acc = acc + ext_ref[POOL_HALO - back:POOL_HALO - back + ts, cols]
        pooled = acc / jnp.minimum(pos, float(win)) - cur
        y = _dot(pooled.astype(BF16), w_ref[g])
        o_ref[0, :, cols] = (y * scale_ref[:, cols]).astype(o_ref.dtype)
    ext_ref[0:POOL_HALO, :] = ext_ref[ts:ts + POOL_HALO, :]


def _pool(proj_f32, pool_w, pool_scale, b, s, ts=512):
    n_pool, group, _ = pool_w.shape
    width = n_pool * group
    ts = min(ts, s)
    return pl.pallas_call(
        functools.partial(_pool_body, ts=ts, group=group),
        grid=(b, s // ts),
        in_specs=[pl.BlockSpec((1, ts, width), lambda i, j: (i, j, 0)),
                  pl.BlockSpec((n_pool, group, group), lambda i, j: (0, 0, 0)),
                  pl.BlockSpec((1, width), lambda i, j: (0, 0))],
        out_specs=pl.BlockSpec((1, ts, width), lambda i, j: (i, j, 0)),
        out_shape=jax.ShapeDtypeStruct((b, s, width), BF16),
        scratch_shapes=[pltpu.VMEM((ts + POOL_HALO, width), F32)],
        compiler_params=_params(("arbitrary", "arbitrary"), 32),
        name="pool_mixer",
    )(proj_f32, pool_w, pool_scale.reshape(1, width))


def _cumf_body(x_ref, bias_ref, o_ref, carry_ref, *, ts):
    @pl.when(pl.program_id(1) == 0)
    def _():
        carry_ref[...] = jnp.zeros_like(carry_ref)

    z = x_ref[0] + bias_ref[...]
    log_f = jnp.minimum(z, 0.0) - jnp.log1p(jnp.exp(-jnp.abs(z)))
    row = lax.broadcasted_iota(jnp.int32, (ts, ts), 0)
    col = lax.broadcasted_iota(jnp.int32, (ts, ts), 1)
    tril = jnp.where(row >= col, 1.0, 0.0).astype(BF16)
    cs = _dot01(tril, log_f) + carry_ref[...]
    o_ref[0] = cs
    carry_ref[...] = cs[ts - 1:ts, :]


def _cumf(small, bias_row, ts=512):
    b, s, n = small.shape
    ts = min(ts, s)
    return pl.pallas_call(
        functools.partial(_cumf_body, ts=ts),
        grid=(b, s // ts),
        in_specs=[pl.BlockSpec((1, ts, n), lambda i, j: (i, j, 0)),
                  pl.BlockSpec((1, n), lambda i, j: (0, 0))],
        out_specs=pl.BlockSpec((1, ts, n), lambda i, j: (i, j, 0)),
        out_shape=jax.ShapeDtypeStruct((b, s, n), F32),
        scratch_shapes=[pltpu.VMEM((1, n), F32)],
        compiler_params=_params(("arbitrary", "arbitrary"), 32),
        name="cum_forget",
    )(small, bias_row)


DN_HEADS_PER_STEP = 2
PAIR = 2 * CHUNK


def _dn_pairs_body(q_ref, k_ref, v_ref, wq_ref, wk_ref, wv_ref, small_ref, alog_ref, dtb_ref,
                   onw_ref, gate_ref, o_ref, state_ref, eq_ref, ek_ref, ev_ref, *, ts):
    halo = SUBLANES
    hg = DN_HEADS_PER_STEP
    n_pairs = ts // PAIR

    @pl.when(pl.program_id(2) == 0)
    def _():
        state_ref[...] = jnp.zeros_like(state_ref)
        for e_ref in (eq_ref, ek_ref, ev_ref):
            e_ref[0:halo, :] = jnp.zeros((halo, hg * HEAD_DIM), F32)

    def conv_silu(x_ref, e_ref, w_ref):
        e_ref[halo:halo + ts, :] = x_ref[0]
        w = w_ref[...]
        y = e_ref[halo:halo + ts, :] * w[CONV_K - 1:CONV_K, :]
        for back in range(1, CONV_K):
            tap = CONV_K - 1 - back
            y = y + e_ref[halo - back:halo - back + ts, :] * w[tap:tap + 1, :]
        e_ref[0:halo, :] = e_ref[ts:ts + halo, :]
        return y * _sigmoid(y)

    def l2n(x):
        return x * lax.rsqrt(jnp.sum(x * x, axis=-1, keepdims=True) + EPS)

    qc = conv_silu(q_ref, eq_ref, wq_ref)
    kc = conv_silu(k_ref, ek_ref, wk_ref)
    vc = conv_silu(v_ref, ev_ref, wv_ref)

    ri = lax.broadcasted_iota(jnp.int32, (PAIR, PAIR), 0)
    ci = lax.broadcasted_iota(jnp.int32, (PAIR, PAIR), 1)
    same = (ri >= CHUNK) == (ci >= CHUNK)
    incl = jnp.logical_and(same, ri >= ci)
    strict = jnp.logical_and(same, ri > ci)
    eye = jnp.where(ri == ci, 1.0, 0.0)
    tril01 = jnp.where(incl, 1.0, 0.0).astype(BF16)
    first = lax.broadcasted_iota(jnp.int32, (PAIR, 1), 0) < CHUNK

    src = lax.broadcasted_iota(jnp.int32, (LANES, LANES), 0)
    dst = lax.broadcasted_iota(jnp.int32, (LANES, LANES), 1)
    head0 = pl.program_id(1) * hg
    wanted = jnp.where(dst < hg, head0 + dst, N_HEADS + head0 + dst - hg)
    pick = jnp.where(jnp.logical_and(dst < 2 * hg, src == wanted), 1.0, 0.0).astype(BF16)
    logits = _pick_columns(small_ref[0], pick)

    qs, ks, vs, betas, gbs = [], [], [], [], []
    for h in range(hg):
        cols = slice(h * HEAD_DIM, (h + 1) * HEAD_DIM)
        q_h = l2n(qc[:, cols]) * (HEAD_DIM ** -0.5)
        k_h = l2n(kc[:, cols])
        v_h = vc[:, cols]
        beta_h = _sigmoid(logits[:, h:h + 1])
        g_h = -jnp.exp(alog_ref[h]) * _softplus(logits[:, hg + h:hg + h + 1] + dtb_ref[h])
        for p in range(n_pairs):
            rows = slice(p * PAIR, (p + 1) * PAIR)
            qs.append(q_h[rows])
            ks.append(k_h[rows])
            vs.append(v_h[rows])
            betas.append(beta_h[rows])
            gbs.append(jnp.broadcast_to(g_h[rows], (PAIR, HEAD_DIM)))

    def bf(x):
        return x.astype(BF16)

    gcum = [_dot01(tril01, gb) for gb in gbs]
    decay = [jnp.exp(jnp.where(incl, gc - gc.T, NEG_INF)) for gc in gcum]
    kb = [k * beta for k, beta in zip(ks, betas)]
    vb = [v * beta for v, beta in zip(vs, betas)]
    qk = [_dot_nt(bf(jnp.concatenate([q, kbi], axis=0)), bf(k)) for q, kbi, k in zip(qs, kb, ks)]
    a_intra = [m[:PAIR] * d for m, d in zip(qk, decay)]
    x = [-jnp.where(strict, m[PAIR:] * d, 0.0) for m, d in zip(qk, decay)]
    t_mat = [eye + xi for xi in x]
    pw = [_dot(bf(xi), bf(xi)) for xi in x]
    for _ in range(4):
        tp = [_dot(bf(jnp.concatenate([t, p], axis=0)), bf(p)) for t, p in zip(t_mat, pw)]
        t_mat = [t + m[:PAIR] for t, m in zip(t_mat, tp)]
        pw = [m[PAIR:] for m in tp]
    t_mat = [t + _dot(bf(t), bf(p)) for t, p in zip(t_mat, pw)]
    eg = [jnp.exp(gc) for gc in gcum]
    uw = [_dot(bf(t), bf(jnp.concatenate([vbi, kbi * e], axis=1)))
          for t, vbi, kbi, e in zip(t_mat, vb, kb, eg)]
    aw = [_dot(bf(a), bf(m)) for a, m in zip(a_intra, uw)]
    q_eff = [q * e - m[:, HEAD_DIM:] for q, e, m in zip(qs, eg, aw)]
    g_last = [jnp.where(first, gc[CHUNK - 1:CHUNK], gc[PAIR - 1:PAIR]) for gc in gcum]
    k_dec = [k * jnp.exp(gl - gc) for k, gl, gc in zip(ks, g_last, gcum)]
    k_split = [jnp.concatenate([jnp.where(first, kd, 0.0), jnp.where(first, 0.0, kd)], axis=1)
               for kd in k_dec]
    kw = [_dot_tn(bf(kd2), bf(m)) for kd2, m in zip(k_split, uw)]

    onw = onw_ref[...]
    states = [state_ref[h] for h in range(hg)]
    outs = [[] for _ in range(hg)]
    for p in range(n_pairs):
        for c in range(2):
            rows = slice(c * CHUNK, (c + 1) * CHUNK)
            krows = slice(c * HEAD_DIM, (c + 1) * HEAD_DIM)
            for h in range(hg):
                i = h * n_pairs + p
                lhs = jnp.concatenate([-kw[i][krows, HEAD_DIM:], q_eff[i][rows]], axis=0)
                res = _dot(bf(lhs), bf(states[h]))
                outs[h].append(res[HEAD_DIM:] + aw[i][rows, :HEAD_DIM])
                gamma = jnp.exp(gcum[i][(c + 1) * CHUNK - 1:(c + 1) * CHUNK])
                states[h] = states[h] * gamma + res[:HEAD_DIM] + kw[i][krows, :HEAD_DIM]
    for h in range(hg):
        cols = slice(h * HEAD_DIM, (h + 1) * HEAD_DIM)
        state_ref[h] = states[h]
        o = jnp.concatenate(outs[h], axis=0)
        o = o * lax.rsqrt(jnp.mean(o * o, axis=-1, keepdims=True) + EPS) * onw
        gate = gate_ref[0, :, cols].astype(F32)
        o_ref[0, :, cols] = (o * (gate * _sigmoid(gate))).astype(o_ref.dtype)


def _deltanet(proj_f32, qkv_col0, proj_bf16, gate_col0, small, conv_w, a_log, dt_bias,
              onorm_w, b, s, ts=512):
    ts = min(ts, s)
    hg = DN_HEADS_PER_STEP
    width = N_HEADS * HEAD_DIM
    wb = hg * HEAD_DIM
    n_groups = N_HEADS // hg

    def col_spec(col0):
        return pl.BlockSpec((1, ts, wb), lambda i, h, j: (i, j, col0 // wb + h))

    def conv_spec(col0):
        return pl.BlockSpec((CONV_K, wb), lambda i, h, j: (0, col0 // wb + h))

    narrow = pl.BlockSpec((1, ts, LANES), lambda i, h, j: (i, j, 0))
    scalar = pl.BlockSpec((hg, 1, 1), lambda i, h, j: (h, 0, 0))
    return pl.pallas_call(
        functools.partial(_dn_pairs_body, ts=ts),
        grid=(b, n_groups, s // ts),
        in_specs=[col_spec(qkv_col0), col_spec(qkv_col0 + width), col_spec(qkv_col0 + 2 * width),
                  conv_spec(0), conv_spec(width), conv_spec(2 * width),
                  narrow, scalar, scalar,
                  pl.BlockSpec((1, HEAD_DIM), lambda i, h, j: (0, 0)),
                  col_spec(gate_col0)],
        out_specs=pl.BlockSpec((1, ts, wb), lambda i, h, j: (i, j, h)),
        out_shape=jax.ShapeDtypeStruct((b, s, width), BF16),
        scratch_shapes=[pltpu.VMEM((hg, HEAD_DIM, HEAD_DIM), F32)] +
                       [pltpu.VMEM((ts + SUBLANES, wb), F32)] * 3,
        compiler_params=_params(("arbitrary", "arbitrary", "arbitrary"), 32),
        name="deltanet",
    )(proj_f32, proj_f32, proj_f32, conv_w, conv_w, conv_w, small,
      a_log.reshape(N_HEADS, 1, 1), dt_bias.reshape(N_HEADS, 1, 1),
      onorm_w.reshape(1, HEAD_DIM), proj_bf16)


LOG2E = 1.4426950408889634
FOX_AUG = 2 * HEAD_DIM


def _fox_body(q_ref, k_ref, v_ref, cumq_ref, cum_ref, o_ref, kaug_ref, vt_ref, *, tq, s_len,
              cum_col0):
    qi = pl.program_id(2)
    hg = FOX_HEADS_PER_STEP
    heads = range(hg)
    head0 = pl.program_id(1) * hg

    @pl.when(qi == 0)
    def _():
        src = lax.broadcasted_iota(jnp.int32, (LANES, LANES), 0)
        dst = lax.broadcasted_iota(jnp.int32, (LANES, LANES), 1)
        pick = jnp.where(jnp.logical_and(dst < hg, src == cum_col0 + head0 + dst), 1.0, 0.0)
        cum_keys = _pick_columns(cum_ref[0], pick.astype(BF16)) * LOG2E
        lane = lax.broadcasted_iota(jnp.int32, (s_len, HEAD_DIM), 1)
        for h in heads:
            cols = slice(h * HEAD_DIM, (h + 1) * HEAD_DIM)
            hi, mid, lo = _split3(cum_keys[:, h:h + 1])
            pieces = jnp.where(lane == 0, hi.astype(F32),
                               jnp.where(lane == 1, mid.astype(F32),
                                         jnp.where(lane == 2, lo.astype(F32), 0.0)))
            kaug_ref[h, :, :HEAD_DIM] = k_ref[0, :, cols]
            kaug_ref[h, :, HEAD_DIM:] = pieces.astype(BF16)
            for t in range(s_len // tq):
                rows = slice(t * tq, (t + 1) * tq)
                vt_ref[h, :, rows] = v_ref[0, rows, cols].astype(F32).T.astype(BF16)

    lane = lax.broadcasted_iota(jnp.int32, (tq, HEAD_DIM), 1)
    minus_one = jnp.where(lane < 3, -1.0, 0.0).astype(BF16)
    q_aug = [jnp.concatenate(
        [(q_ref[0, :, h * HEAD_DIM:(h + 1) * HEAD_DIM].astype(F32)
          * (HEAD_DIM ** -0.5 * LOG2E)).astype(BF16), minus_one], axis=1) for h in heads]
    row = lax.broadcasted_iota(jnp.int32, (SUBLANES, LANES), 0)
    col = lax.broadcasted_iota(jnp.int32, (SUBLANES, LANES), 1)
    pick_rows = jnp.where(jnp.logical_and(row < hg, col == cum_col0 + head0 + row), 1.0, 0.0)
    p_hi, p_mid, p_lo = (_dot_nt(pick_rows.astype(BF16), piece) for piece in _split3(cumq_ref[0]))
    cum_q = ((p_hi + p_mid) + p_lo) * LOG2E
    cq = [cum_q[h:h + 1] for h in heads]

    def step(ki, carry, diagonal):
        ms, ls, accs = carry
        start = pl.multiple_of(ki * tq, tq)
        s_t = [_dot_nt(kaug_ref[h, pl.ds(start, tq), :], q_aug[h]) for h in heads]
        if diagonal:
            key = lax.broadcasted_iota(jnp.int32, (tq, tq), 0)
            qry = lax.broadcasted_iota(jnp.int32, (tq, tq), 1)
            s_t = [jnp.where(key <= qry, s, NEG_INF) for s in s_t]
        m_new = [jnp.maximum(ms[h], jnp.max(s_t[h], axis=0, keepdims=True) + cq[h]) for h in heads]
        p_t = [jnp.exp2(s_t[h] - (m_new[h] - cq[h])) for h in heads]
        alpha = [jnp.exp2(ms[h] - m_new[h]) for h in heads]
        l_new = [alpha[h] * ls[h] + jnp.sum(p_t[h], axis=0, keepdims=True) for h in heads]
        acc_new = [alpha[h] * accs[h] + _dot(vt_ref[h, :, pl.ds(start, tq)], p_t[h].astype(BF16))
                   for h in heads]
        return tuple(m_new), tuple(l_new), tuple(acc_new)

    init = (tuple(jnp.full((1, tq), NEG_INF, F32) for _ in heads),
            tuple(jnp.zeros((1, tq), F32) for _ in heads),
            tuple(jnp.zeros((HEAD_DIM, tq), F32) for _ in heads))
    carry = lax.fori_loop(0, qi, functools.partial(step, diagonal=False), init)
    _, ls, accs = step(qi, carry, diagonal=True)
    for h in heads:
        o_ref[0, :, h * HEAD_DIM:(h + 1) * HEAD_DIM] = (accs[h] / ls[h]).T.astype(o_ref.dtype)


FOX_HEADS_PER_STEP = 4


def _fox(proj_bf16, qkv_col0, cum, cum_col0, b, s, tq=512):
    tq = min(tq, s)
    hg = FOX_HEADS_PER_STEP
    width = N_HEADS * HEAD_DIM
    wb = hg * HEAD_DIM
    return pl.pallas_call(
        functools.partial(_fox_body, tq=tq, s_len=s, cum_col0=cum_col0),
        grid=(b, N_HEADS // hg, s // tq),
        in_specs=[pl.BlockSpec((1, tq, wb), lambda i, h, j: (i, j, qkv_col0 // wb + h)),
                  pl.BlockSpec((1, s, wb), lambda i, h, j: (i, 0, (qkv_col0 + width) // wb + h)),
                  pl.BlockSpec((1, s, wb), lambda i, h, j: (i, 0, (qkv_col0 + 2 * width) // wb + h)),
                  pl.BlockSpec((1, tq, LANES), lambda i, h, j: (i, j, 0)),
                  pl.BlockSpec((1, s, LANES), lambda i, h, j: (i, 0, 0))],
        out_specs=pl.BlockSpec((1, tq, wb), lambda i, h, j: (i, j, h)),
        out_shape=jax.ShapeDtypeStruct((b, s, width), BF16),
        scratch_shapes=[pltpu.VMEM((hg, s, FOX_AUG), BF16), pltpu.VMEM((hg, HEAD_DIM, s), BF16)],
        compiler_params=_params(("arbitrary", "arbitrary", "arbitrary"), 48),
        name="fox_attention",
    )(proj_bf16, proj_bf16, proj_bf16, cum, cum)


def _merge_body(yp_ref, yd_ref, yf_ref, wp_ref, wd_ref, wf_ref, g0_ref, g1_ref, g2_ref, o_ref):
    acc = g0_ref[...].astype(F32) * _dot(yp_ref[...], wp_ref[...])
    acc = acc + g1_ref[...].astype(F32) * _dot(yd_ref[...], wd_ref[...])
    acc = acc + g2_ref[...].astype(F32) * _dot(yf_ref[...], wf_ref[...])
    o_ref[...] = acc.astype(o_ref.dtype)


def _merge(y_pool, y_dn, y_fox, w_pool, w_dn, w_fox, gates, tm=512, tn=1024):
    m, k = y_pool.shape
    n = w_pool.shape[1]
    tm = min(tm, m)
    nb = n // tn
    y_spec = pl.BlockSpec((tm, k), lambda i, j: (i, 0))
    w_spec = pl.BlockSpec((k, tn), lambda i, j: (0, j))

    def gate_spec(branch):
        return pl.BlockSpec((tm, tn), lambda i, j: (i, branch * nb + j))

    return pl.pallas_call(
        _merge_body,
        grid=(m // tm, nb),
        in_specs=[y_spec, y_spec, y_spec, w_spec, w_spec, w_spec,
                  gate_spec(0), gate_spec(1), gate_spec(2)],
        out_specs=pl.BlockSpec((tm, tn), lambda i, j: (i, j)),
        out_shape=jax.ShapeDtypeStruct((m, n), BF16),
        compiler_params=_params(("arbitrary", "arbitrary"), 44),
        name="branch_merge",
    )(y_pool, y_dn, y_fox, w_pool, w_dn, w_fox, gates, gates, gates)


def _out_proj_body(a_ref, w_ref, x_ref, g_ref, nw_ref, sc_ref, sh_ref, x_out_ref, h_ref, ht_ref):
    x = x_ref[0] + g_ref[0] * _dot(a_ref[0], w_ref[...])
    x_out_ref[0] = x
    y = _rms(x, nw_ref[...]) * (1.0 + sc_ref[0]) + sh_ref[0]
    h_ref[0] = y.astype(h_ref.dtype)
    ht_ref[...] = y.T.astype(ht_ref.dtype)


def _out_proj(merged, w_out, x, gate, norm_w, sc, sh, tm=512):
    b, s, d = x.shape
    k = merged.shape[-1]
    tm = min(tm, s)
    ns = s // tm
    tile = pl.BlockSpec((1, tm, d), lambda i, j: (i, j, 0))
    per_batch = pl.BlockSpec((1, 1, d), lambda i, j: (i, 0, 0))
    return pl.pallas_call(
        _out_proj_body,
        grid=(b, ns),
        in_specs=[pl.BlockSpec((1, tm, k), lambda i, j: (i, j, 0)),
                  pl.BlockSpec((k, d), lambda i, j: (0, 0), pipeline_mode=pl.Buffered(1)),
                  tile, per_batch,
                  pl.BlockSpec((1, d), lambda i, j: (0, 0)), per_batch, per_batch],
        out_specs=[tile, tile, pl.BlockSpec((d, tm), lambda i, j: (0, i * ns + j))],
        out_shape=[jax.ShapeDtypeStruct((b, s, d), F32), jax.ShapeDtypeStruct((b, s, d), BF16),
                   jax.ShapeDtypeStruct((d, b * s), BF16)],
        compiler_params=_params(("arbitrary", "arbitrary"), 48),
        name="out_proj_residual",
    )(merged, w_out, x, gate, norm_w.reshape(1, d), sc, sh)


_CANDIDATES = [(a, b) for a in range(PEER_TOPK) for b in range(PEER_TOPK)
               if (a + 1) * (b + 1) <= PEER_TOPK]
_CROSS_PAIRS = [(c, d) for c in _CANDIDATES for d in _CANDIDATES if c[0] < d[0] and c[1] > d[1]]
_SELECT_INTERLEAVE = 4


def _peer_select_body(h_ref, wq_ref, sk_ref, e1_ref, cnt_ref, e2_ref, rank2_ref,
                      sc_ref, rk_ref, tp_ref, cn_ref, zi_ref, *, tt):
    n_lt = tt // LANES
    n_half = PEER_HEADS * n_lt
    q = _dot(h_ref[...], wq_ref[...])
    for hp in range(2 * PEER_HEADS):
        head, half = divmod(hp, 2)
        q_hp = q[:, hp * N_SUBKEYS:(hp + 1) * N_SUBKEYS].astype(BF16)
        s_t = _dot_nt(sk_ref[hp], q_hp)
        for lt in range(n_lt):
            sc_ref[half * n_half + head * n_lt + lt] = s_t[:, lt * LANES:(lt + 1) * LANES]

    key_id = lax.broadcasted_iota(jnp.int32, (N_SUBKEYS, LANES), 0).astype(F32)

    def half_topk(it, carry):
        base = it * _SELECT_INTERLEAVE
        vs = [sc_ref[base + k] for k in range(_SELECT_INTERLEAVE)]
        ranks = [jnp.full((N_SUBKEYS, LANES), float(PEER_TOPK), F32)] * _SELECT_INTERLEAVE
        for r in range(PEER_TOPK):
            ms = [jnp.max(v, axis=0, keepdims=True) for v in vs]
            firsts = [jnp.min(jnp.where(v == m, key_id, float(N_SUBKEYS)), axis=0, keepdims=True)
                      for v, m in zip(vs, ms)]
            hits = [key_id == first for first in firsts]
            ranks = [jnp.where(hit, float(r), rank) for hit, rank in zip(hits, ranks)]
            vs = [jnp.where(hit, NEG_INF, v) for hit, v in zip(hits, vs)]
            for k in range(_SELECT_INTERLEAVE):
                tp_ref[r, pl.ds(base + k, 1), :] = ms[k]
        for k in range(_SELECT_INTERLEAVE):
            rk_ref[base + k] = ranks[k]
        return carry

    lax.fori_loop(0, 2 * n_half // _SELECT_INTERLEAVE, half_topk, 0)

    def pair_topk(g, carry):
        b1 = pl.multiple_of(g * SUBLANES, SUBLANES)
        b2 = b1 + n_half
        t1 = [tp_ref[a, pl.ds(b1, SUBLANES), :] for a in range(PEER_TOPK)]
        t2 = [tp_ref[b, pl.ds(b2, SUBLANES), :] for b in range(PEER_TOPK)]
        cand = {(a, b): t1[a] + t2[b] for a, b in _CANDIDATES}
        beaten = {(a, b): jnp.full((SUBLANES, LANES), float((a + 1) * (b + 1) - 1), F32)
                  for a, b in _CANDIDATES}
        for c, d in _CROSS_PAIRS:
            c_first = cand[c] >= cand[d]
            beaten[d] = beaten[d] + jnp.where(c_first, 1.0, 0.0)
            beaten[c] = beaten[c] + jnp.where(c_first, 0.0, 1.0)
        z = jnp.zeros((SUBLANES, LANES), F32)
        cnts = [jnp.zeros((SUBLANES, LANES), F32) for _ in range(PEER_TOPK)]
        for a, b in _CANDIDATES:
            sel = jnp.where(beaten[(a, b)] < float(PEER_TOPK), 1.0, 0.0)
            cnts[a] = cnts[a] + sel
            z = z + sel * jnp.exp(cand[(a, b)] - cand[(0, 0)])
        for a in range(PEER_TOPK):
            cn_ref[a, pl.ds(b1, SUBLANES), :] = cnts[a]
        zi_ref[pl.ds(b1, SUBLANES), :] = 1.0 / z
        return carry

    lax.fori_loop(0, n_half // SUBLANES, pair_topk, 0)

    def emit(i1, carry):
        i2 = i1 + n_half
        head = i1 // n_lt
        lt = i1 % n_lt
        rank1 = rk_ref[i1]
        cnt = jnp.zeros((N_SUBKEYS, LANES), F32)
        for a in range(PEER_TOPK):
            cnt = jnp.where(rank1 == float(a), cn_ref[a, pl.ds(i1, 1), :], cnt)
        e1 = jnp.exp(sc_ref[i1] - tp_ref[0, pl.ds(i1, 1), :]) * zi_ref[pl.ds(i1, 1), :]
        e1_ref[head, lt] = e1
        cnt_ref[head, lt] = cnt
        e2 = jnp.exp(sc_ref[i2] - tp_ref[0, pl.ds(i2, 1), :])
        e2_ref[head, lt] = pltpu.bitcast(e2.astype(BF16), jnp.uint32)
        rank2_ref[head, lt] = pltpu.bitcast(rk_ref[i2].astype(BF16), jnp.uint32)
        return carry

    lax.fori_loop(0, n_half, emit, 0)


def _peer_select(h, w_q, sub_keys, tt=512):
    t, d = h.shape
    tt = min(tt, t)
    n_lt = tt // LANES
    n_prob = 2 * PEER_HEADS * n_lt
    dims = (PEER_HEADS, t // LANES, N_SUBKEYS, LANES)
    packed_dims = (PEER_HEADS, t // LANES, N_SUBKEYS // 2, LANES)
    spec = pl.BlockSpec((PEER_HEADS, n_lt, N_SUBKEYS, LANES), lambda i: (0, i, 0, 0))
    packed_spec = pl.BlockSpec((PEER_HEADS, n_lt, N_SUBKEYS // 2, LANES), lambda i: (0, i, 0, 0))
    return pl.pallas_call(
        functools.partial(_peer_select_body, tt=tt),
        grid=(t // tt,),
        in_specs=[pl.BlockSpec((tt, d), lambda i: (i, 0)),
                  pl.BlockSpec(w_q.shape, lambda i: (0, 0)),
                  pl.BlockSpec(sub_keys.shape, lambda i: (0, 0, 0))],
        out_specs=[spec, spec, packed_spec, packed_spec],
        out_shape=[jax.ShapeDtypeStruct(dims, F32), jax.ShapeDtypeStruct(dims, F32),
                   jax.ShapeDtypeStruct(packed_dims, jnp.uint32),
                   jax.ShapeDtypeStruct(packed_dims, jnp.uint32)],
        scratch_shapes=[pltpu.VMEM((n_prob, N_SUBKEYS, LANES), F32),
                        pltpu.VMEM((n_prob, N_SUBKEYS, LANES), F32),
                        pltpu.VMEM((PEER_TOPK, n_prob, LANES), F32),
                        pltpu.VMEM((PEER_TOPK, n_prob // 2, LANES), F32),
                        pltpu.VMEM((n_prob // 2, LANES), F32)],
        compiler_params=_params(("arbitrary",), 48),
        name="peer_select",
    )(h, w_q, sub_keys)


def _peer_dense_body(u0_ref, ht0_ref, u_ref, ht_ref, vt_ref, e1_ref, cnt_ref, e2_ref, rank2_ref,
                     o_ref, act_a_ref, act_b_ref, coef_ref, *, tt, te):
    n_lt = tt // LANES
    n_i = te // N_SUBKEYS
    bf16_rows = 2 * SUBLANES
    zero = jnp.zeros((), BF16)
    step = pl.program_id(0) * pl.num_programs(1) + pl.program_id(1)

    def row_bf16(ref, head, lt, ii):
        row = jnp.broadcast_to(ref[head, lt, ii:ii + 1, :], (bf16_rows, LANES)).astype(BF16)
        return jnp.tile(row, (N_SUBKEYS // bf16_rows, 1))

    @pl.when(step == 0)
    def _():
        act_a_ref[...] = _dot(u0_ref[...], ht0_ref[...])

    @pl.when(pl.program_id(1) == 0)
    def _():
        o_ref[...] = jnp.zeros_like(o_ref)

    def work(act_ref, next_act_ref):
        next_act_ref[...] = _dot(u_ref[...], ht_ref[...])
        for ii in range(n_i):
            rows = slice(ii * N_SUBKEYS, (ii + 1) * N_SUBKEYS)
            for lt in range(n_lt):
                lanes = slice(lt * LANES, (lt + 1) * LANES)
                w = None
                for head in range(PEER_HEADS):
                    cnt = row_bf16(cnt_ref, head, lt, ii)
                    e1 = row_bf16(e1_ref, head, lt, ii)
                    rank2 = pltpu.bitcast(rank2_ref[head, lt], BF16)
                    e2 = pltpu.bitcast(e2_ref[head, lt], BF16)
                    term = jnp.where(rank2 < cnt, e2, zero) * e1
                    w = term if w is None else w + term
                a = act_ref[rows, lanes]
                gelu = 0.5 * a * (1.0 + lax.erf(a * (2.0 ** -0.5)))
                coef_ref[rows, lanes] = gelu.astype(BF16) * w
        o_ref[...] += _dot(vt_ref[...], coef_ref[...])

    @pl.when(lax.rem(step, 2) == 0)
    def _():
        work(act_a_ref, act_b_ref)

    @pl.when(lax.rem(step, 2) == 1)
    def _():
        work(act_b_ref, act_a_ref)


def _peer_dense(u, h_t, v_t, e1, cnt, e2, rank2, tt=512, te=1024):
    n_exp, d = u.shape
    t = h_t.shape[1]
    tt = min(tt, t)
    n_lt = tt // LANES
    n_i = te // N_SUBKEYS
    ni, nj = t // tt, n_exp // te
    full = pl.BlockSpec((PEER_HEADS, n_lt, N_SUBKEYS // 2, LANES), lambda i, j: (0, i, 0, 0))
    rows = pl.BlockSpec((PEER_HEADS, n_lt, n_i, LANES), lambda i, j: (0, i, j, 0))
    once = pl.Buffered(1)

    def next_token_tile(i, j):
        return jnp.minimum(i + (j + 1) // nj, ni - 1)

    return pl.pallas_call(
        functools.partial(_peer_dense_body, tt=tt, te=te),
        grid=(ni, nj),
        in_specs=[pl.BlockSpec((te, d), lambda i, j: (0, 0), pipeline_mode=once),
                  pl.BlockSpec((d, tt), lambda i, j: (0, 0), pipeline_mode=once),
                  pl.BlockSpec((te, d), lambda i, j: ((j + 1) % nj, 0)),
                  pl.BlockSpec((d, tt), lambda i, j: (0, next_token_tile(i, j))),
                  pl.BlockSpec((d, te), lambda i, j: (0, j)),
                  rows, rows, full, full],
        out_specs=pl.BlockSpec((d, tt), lambda i, j: (0, i)),
        out_shape=jax.ShapeDtypeStruct((d, t), F32),
        scratch_shapes=[pltpu.VMEM((te, tt), F32), pltpu.VMEM((te, tt), F32),
                        pltpu.VMEM((te, tt), BF16)],
        compiler_params=_params(("arbitrary", "arbitrary"), 50),
        name="peer_dense",
    )(u, h_t, u, h_t, v_t, e1, cnt, e2, rank2)


def _peer_residual_body(x_ref, yt_ref, g_ref, w_ref, sc_ref, sh_ref, x_out_ref, h_ref):
    x = x_ref[0] + g_ref[0] * yt_ref[...].T
    x_out_ref[0] = x
    h_ref[0] = (_rms(x, w_ref[...]) * (1.0 + sc_ref[0]) + sh_ref[0]).astype(h_ref.dtype)


def _peer_residual_final_body(x_ref, yt_ref, g_ref, w_ref, o_ref):
    o_ref[0] = _rms(x_ref[0] + g_ref[0] * yt_ref[...].T, w_ref[...])


def _peer_residual(x, y_t, gate, norm_w, sc=None, sh=None, ts=512):
    b, s, d = x.shape
    ts = min(ts, s)
    ns = s // ts
    tile = pl.BlockSpec((1, ts, d), lambda i, j: (i, j, 0))
    per_batch = pl.BlockSpec((1, 1, d), lambda i, j: (i, 0, 0))
    in_specs = [tile, pl.BlockSpec((d, ts), lambda i, j: (0, i * ns + j)), per_batch,
                pl.BlockSpec((1, d), lambda i, j: (0, 0))]
    args = [x, y_t, gate, norm_w.reshape(1, d)]
    if sc is None:
        body, out_specs, out_shape = _peer_residual_final_body, tile, jax.ShapeDtypeStruct((b, s, d), F32)
    else:
        body = _peer_residual_body
        in_specs += [per_batch, per_batch]
        args += [sc, sh]
        out_specs = [tile, tile]
        out_shape = [jax.ShapeDtypeStruct((b, s, d), F32), jax.ShapeDtypeStruct((b, s, d), BF16)]
    return pl.pallas_call(
        body,
        grid=(b, ns),
        in_specs=in_specs,
        out_specs=out_specs,
        out_shape=out_shape,
        compiler_params=_params(("arbitrary", "arbitrary"), 44),
        name="peer_residual",
    )(*args)


def kernel(x, c, ada_w, ada_b, norm_mix_w, w_in, pool_w, pool_scale, dn_conv_w, dn_a_log,
           dn_dt_bias, dn_onorm_w, fox_f_bias, w_branch_pool, w_branch_dn, w_branch_fox,
           w_out, norm_ffn_w, peer_w_q, peer_sub_keys, peer_u, peer_v, final_norm_w):
    b, s, d = x.shape
    t = b * s
    depth = ada_w.shape[0]
    pool_width = pool_w.shape[1] * pool_w.shape[2]
    width = N_HEADS * HEAD_DIM
    nh = N_HEADS

    c8 = jnp.zeros((SUBLANES, d), F32).at[:b].set(c)
    mod = _ada(c8, ada_w, ada_b)[:, :b].reshape(depth, b, 6, 1, d)

    o_pool = 0
    o_dnqkv = o_pool + pool_width
    o_dnb = o_dnqkv + 3 * width
    o_dna = o_dnb + nh
    o_dng = o_dna + nh
    o_foxqkv = o_dng + width
    o_foxf = o_foxqkv + 3 * width
    o_gates = o_foxf + nh

    for l in range(depth):
        sh_m, sc_m, g_m, sh_f, sc_f, g_f = (mod[l, :, i] for i in range(6))
        wl = w_in[l]
        w_f32 = wl[:, o_pool:o_dnb].astype(BF16)
        w_b16 = jnp.concatenate([wl[:, o_dng:o_foxqkv], wl[:, o_foxqkv:o_foxf]], axis=1).astype(BF16)
        w_gate = wl[:, o_gates:].astype(BF16)
        w_small = jnp.concatenate(
            [wl[:, o_dnb:o_dng], wl[:, o_foxf:o_gates],
             jnp.zeros((d, LANES - 3 * nh), F32)], axis=1).astype(BF16)

        if l == 0:
            h = _norm_mod(x, norm_mix_w[l], sc_m, sh_m)
        h = h.reshape(t, d)
        proj_f32 = _matmul(h, w_f32, F32).reshape(b, s, -1)
        proj_b16 = _matmul(h, w_b16, BF16).reshape(b, s, -1)
        gates = _matmul(h, w_gate, BF16, sigmoid=True)
        small = _matmul(h, w_small, F32).reshape(b, s, LANES)

        y_pool = _pool(proj_f32, pool_w[l].astype(BF16), pool_scale[l], b, s)

        y_dn = _deltanet(proj_f32, pool_width, proj_b16, 0, small,
                         dn_conv_w[l], dn_a_log[l], dn_dt_bias[l], dn_onorm_w[l], b, s)

        bias_row = jnp.zeros((1, LANES), F32).at[0, 2 * nh:3 * nh].set(fox_f_bias[l])
        cum = _cumf(small, bias_row)
        y_fox = _fox(proj_b16, width, cum, 2 * nh, b, s)

        merged = _merge(y_pool.reshape(t, -1), y_dn.reshape(t, -1), y_fox.reshape(t, -1),
                        w_branch_pool[l].astype(BF16), w_branch_dn[l].astype(BF16),
                        w_branch_fox[l].astype(BF16), gates)
        x, h2, h2_t = _out_proj(merged.reshape(b, s, d), w_out[l].astype(BF16), x, g_m,
                                norm_ffn_w[l], sc_f, sh_f)
        sel = _peer_select(h2.reshape(t, d), peer_w_q[l].astype(BF16),
                           peer_sub_keys[l].reshape(2 * PEER_HEADS, N_SUBKEYS, -1).astype(BF16))
        y_t = _peer_dense(peer_u[l].astype(BF16), h2_t, peer_v[l].T.astype(BF16), *sel)
        if l + 1 < depth:
            x, h = _peer_residual(x, y_t, g_f, norm_mix_w[l + 1], mod[l + 1, :, 1], mod[l + 1, :, 0])
        else:
            out = _peer_residual(x, y_t, g_f, final_norm_w)
    return out
```

```python
import functools

import jax
import jax.numpy as jnp
from jax import lax
from jax.experimental import pallas as pl
from jax.experimental.pallas import tpu as pltpu

F32 = jnp.float32
BF16 = jnp.bfloat16
EPS = 1e-6
MIB = 1024 * 1024

SUBLANES = 8
LANES = 128

CHUNK = 64
CONV_K = 4
POOL_WINDOWS = (2, 4, 8, 16)
POOL_HALO = 16
N_HEADS = 8
HEAD_DIM = 128
PEER_HEADS = 8
N_SUBKEYS = 128
PEER_TOPK = 16
NEG_INF = float("-inf")


def _params(semantics, vmem_mib):
    return pltpu.CompilerParams(dimension_semantics=semantics,
                                vmem_limit_bytes=vmem_mib * MIB)


def _dot(a, b):
    return jnp.dot(a, b, preferred_element_type=F32)


def _dot_nt(a, b):
    return lax.dot_general(a, b, (((1,), (1,)), ((), ())), preferred_element_type=F32)


def _dot_tn(a, b):
    return lax.dot_general(a, b, (((0,), (0,)), ((), ())), preferred_element_type=F32)


def _split3(x):
    hi = x.astype(BF16)
    r1 = x - hi.astype(F32)
    mid = r1.astype(BF16)
    lo = (r1 - mid.astype(F32)).astype(BF16)
    return hi, mid, lo


def _dot01(m01, x):
    n = x.shape[1]
    y = _dot(m01, jnp.concatenate(_split3(x), axis=1))
    return (y[:, :n] + y[:, n:2 * n]) + y[:, 2 * n:]


def _pick_columns(x, pick01):
    m = x.shape[0]
    y = _dot(jnp.concatenate(_split3(x), axis=0), pick01)
    return (y[:m] + y[m:2 * m]) + y[2 * m:]


def _sigmoid(x):
    return 1.0 / (1.0 + jnp.exp(-x))


def _softplus(x):
    return jnp.maximum(x, 0.0) + jnp.log1p(jnp.exp(-jnp.abs(x)))


def _ada_body(c_ref, w_ref, b_ref, o_ref):
    c = c_ref[...]
    cs = (c * _sigmoid(c)).astype(BF16)
    o_ref[0] = _dot(cs, w_ref[0].astype(BF16)) + b_ref[0]


def _ada(c8, ada_w, ada_b, tn=1024):
    depth, d, n = ada_w.shape
    return pl.pallas_call(
        _ada_body,
        grid=(depth, n // tn),
        in_specs=[pl.BlockSpec((SUBLANES, d), lambda l, j: (0, 0)),
                  pl.BlockSpec((1, d, tn), lambda l, j: (l, 0, j)),
                  pl.BlockSpec((1, 1, tn), lambda l, j: (l, 0, j))],
        out_specs=pl.BlockSpec((1, SUBLANES, tn), lambda l, j: (l, 0, j)),
        out_shape=jax.ShapeDtypeStruct((depth, SUBLANES, n), F32),
        compiler_params=_params(("arbitrary", "arbitrary"), 32),
        name="ada_mod",
    )(c8, ada_w, ada_b.reshape(depth, 1, n))


def _rms(x, w):
    ms = jnp.mean(x * x, axis=-1, keepdims=True)
    return x * lax.rsqrt(ms + EPS) * w


def _norm_mod_body(x_ref, w_ref, sc_ref, sh_ref, o_ref):
    y = _rms(x_ref[0], w_ref[...]) * (1.0 + sc_ref[0]) + sh_ref[0]
    o_ref[0] = y.astype(o_ref.dtype)


def _norm_mod(x, w, sc, sh, ts=512):
    b, s, d = x.shape
    ts = min(ts, s)
    return pl.pallas_call(
        _norm_mod_body,
        grid=(b, s // ts),
        in_specs=[pl.BlockSpec((1, ts, d), lambda i, j: (i, j, 0)),
                  pl.BlockSpec((1, d), lambda i, j: (0, 0)),
                  pl.BlockSpec((1, 1, d), lambda i, j: (i, 0, 0)),
                  pl.BlockSpec((1, 1, d), lambda i, j: (i, 0, 0))],
        out_specs=pl.BlockSpec((1, ts, d), lambda i, j: (i, j, 0)),
        out_shape=jax.ShapeDtypeStruct((b, s, d), BF16),
        compiler_params=_params(("arbitrary", "arbitrary"), 40),
        name="norm_mod",
    )(x, w.reshape(1, d), sc, sh)


def _mm_body(a_ref, w_ref, o_ref, *, sigmoid):
    acc = _dot(a_ref[...], w_ref[...])
    if sigmoid:
        acc = _sigmoid(acc)
    o_ref[...] = acc.astype(o_ref.dtype)


def _matmul(a, w, out_dtype, *, sigmoid=False, tm=1024, tn=1024):
    m, k = a.shape
    n = w.shape[1]
    tm, tn = min(tm, m), min(tn, n)
    return pl.pallas_call(
        functools.partial(_mm_body, sigmoid=sigmoid),
        grid=(m // tm, n // tn),
        in_specs=[pl.BlockSpec((tm, k), lambda i, j: (i, 0)),
                  pl.BlockSpec((k, tn), lambda i, j: (0, j))],
        out_specs=pl.BlockSpec((tm, tn), lambda i, j: (i, j)),
        out_shape=jax.ShapeDtypeStruct((m, n), out_dtype),
        compiler_params=_params(("arbitrary", "arbitrary"), 44),
        name="matmul",
    )(a, w)


def _pool_body(p_ref, w_ref, scale_ref, o_ref, ext_ref, *, ts, group):
    j = pl.program_id(1)

    @pl.when(j == 0)
    def _():
        ext_ref[0:POOL_HALO, :] = jnp.zeros((POOL_HALO, ext_ref.shape[1]), F32)

    ext_ref[POOL_HALO:POOL_HALO + ts, :] = p_ref[0]
    pos = (j * ts + lax.broadcasted_iota(jnp.int32, (ts, 1), 0) + 1).astype(F32)
    for g, win in enumerate(POOL_WINDOWS):
        cols = slice(g * group, (g + 1) * group)
        cur = ext_ref[POOL_HALO:POOL_HALO + ts, cols]
        acc = cur
        for back in range(1, win):
            acc = acc + ext_ref[POOL_HALO - back:POOL_HALO - back + ts, cols]
        pooled = acc / jnp.minimum(pos, float(win)) - cur
        y = _dot(pooled.astype(BF16), w_ref[g])
        o_ref[0, :, cols] = (y * scale_ref[:, cols]).astype(o_ref.dtype)
    ext_ref[0:POOL_HALO, :] = ext_ref[ts:ts + POOL_HALO, :]


def _pool(proj_f32, pool_w, pool_scale, b, s, ts=512):
    n_pool, group, _ = pool_w.shape
    width = n_pool * group
    ts = min(ts, s)
    return pl.pallas_call(
        functools.partial(_pool_body, ts=ts, group=group),
        grid=(b, s // ts),
        in_specs=[pl.BlockSpec((1, ts, width), lambda i, j: (i, j, 0)),
                  pl.BlockSpec((n_pool, group, group), lambda i, j: (0, 0, 0)),
                  pl.BlockSpec((1, width), lambda i, j: (0, 0))],
        out_specs=pl.BlockSpec((1, ts, width), lambda i, j: (i, j, 0)),
        out_shape=jax.ShapeDtypeStruct((b, s, width), BF16),
        scratch_shapes=[pltpu.VMEM((ts + POOL_HALO, width), F32)],
        compiler_params=_params(("arbitrary", "arbitrary"), 32),
        name="pool_mixer",
    )(proj_f32, pool_w, pool_scale.reshape(1, width))


def _cumf_body(x_ref, bias_ref, o_ref, carry_ref, *, ts):
    @pl.when(pl.program_id(1) == 0)
    def _():
        carry_ref[...] = jnp.zeros_like(carry_ref)

    z = x_ref[0] + bias_ref[...]
    log_f = jnp.minimum(z, 0.0) - jnp.log1p(jnp.exp(-jnp.abs(z)))
    row = lax.broadcasted_iota(jnp.int32, (ts, ts), 0)
    col = lax.broadcasted_iota(jnp.int32, (ts, ts), 1)
    tril = jnp.where(row >= col, 1.0, 0.0).astype(BF16)
    cs = _dot01(tril, log_f) + carry_ref[...]
    o_ref[0] = cs
    carry_ref[...] = cs[ts - 1:ts, :]


def _cumf(small, bias_row, ts=512):
    b, s, n = small.shape
    ts = min(ts, s)
    return pl.pallas_call(
        functools.partial(_cumf_body, ts=ts),
        grid=(b, s // ts),
        in_specs=[pl.BlockSpec((1, ts, n), lambda i, j: (i, j, 0)),
                  pl.BlockSpec((1, n), lambda i, j: (0, 0))],
        out_specs=pl.BlockSpec((1, ts, n), lambda i, j: (i, j, 0)),
        out_shape=jax.ShapeDtypeStruct((b, s, n), F32),
        scratch_shapes=[pltpu.VMEM((1, n), F32)],
        compiler_params=_params(("arbitrary", "arbitrary"), 32),
        name="cum_forget",
    )(small, bias_row)


DN_HEADS_PER_STEP = 4
PAIR = 2 * CHUNK


def _dn_pairs_body(q_ref, k_ref, v_ref, wq_ref, wk_ref, wv_ref, small_ref, alog_ref, dtb_ref,
                   onw_ref, gate_ref, o_ref, state_ref, eq_ref, ek_ref, ev_ref, *, ts):
    halo = SUBLANES
    hg = DN_HEADS_PER_STEP
    n_pairs = ts // PAIR

    @pl.when(pl.program_id(2) == 0)
    def _():
        state_ref[...] = jnp.zeros_like(state_ref)
        for e_ref in (eq_ref, ek_ref, ev_ref):
            e_ref[0:halo, :] = jnp.zeros((halo, hg * HEAD_DIM), F32)

    def conv_silu(x_ref, e_ref, w_ref):
        e_ref[halo:halo + ts, :] = x_ref[0]
        w = w_ref[...]
        y = e_ref[halo:halo + ts, :] * w[CONV_K - 1:CONV_K, :]
        for back in range(1, CONV_K):
            tap = CONV_K - 1 - back
            y = y + e_ref[halo - back:halo - back + ts, :] * w[tap:tap + 1, :]
        e_ref[0:halo, :] = e_ref[ts:ts + halo, :]
        return y * _sigmoid(y)

    def l2n(x):
        return x * lax.rsqrt(jnp.sum(x * x, axis=-1, keepdims=True) + EPS)

    qc = conv_silu(q_ref, eq_ref, wq_ref)
    kc = conv_silu(k_ref, ek_ref, wk_ref)
    vc = conv_silu(v_ref, ev_ref, wv_ref)

    ri = lax.broadcasted_iota(jnp.int32, (PAIR, PAIR), 0)
    ci = lax.broadcasted_iota(jnp.int32, (PAIR, PAIR), 1)
    same = (ri >= CHUNK) == (ci >= CHUNK)
    incl = jnp.logical_and(same, ri >= ci)
    strict = jnp.logical_and(same, ri > ci)
    eye = jnp.where(ri == ci, 1.0, 0.0)
    tril01 = jnp.where(incl, 1.0, 0.0).astype(BF16)
    first = lax.broadcasted_iota(jnp.int32, (PAIR, 1), 0) < CHUNK

    src = lax.broadcasted_iota(jnp.int32, (LANES, LANES), 0)
    dst = lax.broadcasted_iota(jnp.int32, (LANES, LANES), 1)
    head0 = pl.program_id(1) * hg
    wanted = jnp.where(dst < hg, head0 + dst, N_HEADS + head0 + dst - hg)
    pick = jnp.where(jnp.logical_and(dst < 2 * hg, src == wanted), 1.0, 0.0).astype(BF16)
    logits = _pick_columns(small_ref[0], pick)

    qs, ks, vs, betas, gbs = [], [], [], [], []
    for h in range(hg):
        cols = slice(h * HEAD_DIM, (h + 1) * HEAD_DIM)
        q_h = l2n(qc[:, cols]) * (HEAD_DIM ** -0.5)
        k_h = l2n(kc[:, cols])
        v_h = vc[:, cols]
        beta_h = _sigmoid(logits[:, h:h + 1])
        g_h = -jnp.exp(alog_ref[h]) * _softplus(logits[:, hg + h:hg + h + 1] + dtb_ref[h])
        for p in range(n_pairs):
            rows = slice(p * PAIR, (p + 1) * PAIR)
            qs.append(q_h[rows])
            ks.append(k_h[rows])
            vs.append(v_h[rows])
            betas.append(beta_h[rows])
            gbs.append(jnp.broadcast_to(g_h[rows], (PAIR, HEAD_DIM)))

    def bf(x):
        return x.astype(BF16)

    gcum = [_dot01(tril01, gb) for gb in gbs]
    decay = [jnp.exp(jnp.where(incl, gc - gc.T, NEG_INF)) for gc in gcum]
    kb = [k * beta for k, beta in zip(ks, betas)]
    vb = [v * beta for v, beta in zip(vs, betas)]
    qk = [_dot_nt(bf(jnp.concatenate([q, kbi], axis=0)), bf(k)) for q, kbi, k in zip(qs, kb, ks)]
    a_intra = [m[:PAIR] * d for m, d in zip(qk, decay)]
    x = [-jnp.where(strict, m[PAIR:] * d, 0.0) for m, d in zip(qk, decay)]
    t_mat = [eye + xi for xi in x]
    pw = [_dot(bf(xi), bf(xi)) for xi in x]
    for _ in range(4):
        tp = [_dot(bf(jnp.concatenate([t, p], axis=0)), bf(p)) for t, p in zip(t_mat, pw)]
        t_mat = [t + m[:PAIR] for t, m in zip(t_mat, tp)]
        pw = [m[PAIR:] for m in tp]
    t_mat = [t + _dot(bf(t), bf(p)) for t, p in zip(t_mat, pw)]
    eg = [jnp.exp(gc) for gc in gcum]
    uw = [_dot(bf(t), bf(jnp.concatenate([vbi, kbi * e], axis=1)))
          for t, vbi, kbi, e in zip(t_mat, vb, kb, eg)]
    aw = [_dot(bf(a), bf(m)) for a, m in zip(a_intra, uw)]
    q_eff = [q * e - m[:, HEAD_DIM:] for q, e, m in zip(qs, eg, aw)]
    g_last = [jnp.where(first, gc[CHUNK - 1:CHUNK], gc[PAIR - 1:PAIR]) for gc in gcum]
    k_dec = [k * jnp.exp(gl - gc) for k, gl, gc in zip(ks, g_last, gcum)]
    k_split = [jnp.concatenate([jnp.where(first, kd, 0.0), jnp.where(first, 0.0, kd)], axis=1)
               for kd in k_dec]
    kw = [_dot_tn(bf(kd2), bf(m)) for kd2, m in zip(k_split, uw)]

    onw = onw_ref[...]
    states = [state_ref[h] for h in range(hg)]
    outs = [[] for _ in range(hg)]
    for p in range(n_pairs):
        for c in range(2):
            rows = slice(c * CHUNK, (c + 1) * CHUNK)
            krows = slice(c * HEAD_DIM, (c + 1) * HEAD_DIM)
            for h in range(hg):
                i = h * n_pairs + p
                lhs = jnp.concatenate([-kw[i][krows, HEAD_DIM:], q_eff[i][rows]], axis=0)
                res = _dot(bf(lhs), bf(states[h]))
                outs[h].append(res[HEAD_DIM:] + aw[i][rows, :HEAD_DIM])
                gamma = jnp.exp(gcum[i][(c + 1) * CHUNK - 1:(c + 1) * CHUNK])
                states[h] = states[h] * gamma + res[:HEAD_DIM] + kw[i][krows, :HEAD_DIM]
    for h in range(hg):
        cols = slice(h * HEAD_DIM, (h + 1) * HEAD_DIM)
        state_ref[h] = states[h]
        o = jnp.concatenate(outs[h], axis=0)
        o = o * lax.rsqrt(jnp.mean(o * o, axis=-1, keepdims=True) + EPS) * onw
        gate = gate_ref[0, :, cols].astype(F32)
        o_ref[0, :, cols] = (o * (gate * _sigmoid(gate))).astype(o_ref.dtype)


def _deltanet(proj_f32, qkv_col0, proj_bf16, gate_col0, small, conv_w, a_log, dt_bias,
              onorm_w, b, s, ts=512):
    ts = min(ts, s)
    hg = DN_HEADS_PER_STEP
    width = N_HEADS * HEAD_DIM
    wb = hg * HEAD_DIM
    n_groups = N_HEADS // hg

    def col_spec(col0):
        return pl.BlockSpec((1, ts, wb), lambda i, h, j: (i, j, col0 // wb + h))

    def conv_spec(col0):
        return pl.BlockSpec((CONV_K, wb), lambda i, h, j: (0, col0 // wb + h))

    narrow = pl.BlockSpec((1, ts, LANES), lambda i, h, j: (i, j, 0))
    scalar = pl.BlockSpec((hg, 1, 1), lambda i, h, j: (h, 0, 0))
    return pl.pallas_call(
        functools.partial(_dn_pairs_body, ts=ts),
        grid=(b, n_groups, s // ts),
        in_specs=[col_spec(qkv_col0), col_spec(qkv_col0 + width), col_spec(qkv_col0 + 2 * width),
                  conv_spec(0), conv_spec(width), conv_spec(2 * width),
                  narrow, scalar, scalar,
                  pl.BlockSpec((1, HEAD_DIM), lambda i, h, j: (0, 0)),
                  col_spec(gate_col0)],
        out_specs=pl.BlockSpec((1, ts, wb), lambda i, h, j: (i, j, h)),
        out_shape=jax.ShapeDtypeStruct((b, s, width), BF16),
        scratch_shapes=[pltpu.VMEM((hg, HEAD_DIM, HEAD_DIM), F32)] +
                       [pltpu.VMEM((ts + SUBLANES, wb), F32)] * 3,
        compiler_params=_params(("arbitrary", "arbitrary", "arbitrary"), 32),
        name="deltanet",
    )(proj_f32, proj_f32, proj_f32, conv_w, conv_w, conv_w, small,
      a_log.reshape(N_HEADS, 1, 1), dt_bias.reshape(N_HEADS, 1, 1),
      onorm_w.reshape(1, HEAD_DIM), proj_bf16)


LOG2E = 1.4426950408889634
FOX_AUG = 2 * HEAD_DIM


def _fox_body(q_ref, k_ref, v_ref, cumq_ref, cum_ref, o_ref, kaug_ref, vt_ref, *, tq, s_len,
              cum_col0):
    qi = pl.program_id(2)
    hg = FOX_HEADS_PER_STEP
    heads = range(hg)
    head0 = pl.program_id(1) * hg

    @pl.when(qi == 0)
    def _():
        src = lax.broadcasted_iota(jnp.int32, (LANES, LANES), 0)
        dst = lax.broadcasted_iota(jnp.int32, (LANES, LANES), 1)
        pick = jnp.where(jnp.logical_and(dst < hg, src == cum_col0 + head0 + dst), 1.0, 0.0)
        cum_keys = _pick_columns(cum_ref[0], pick.astype(BF16)) * LOG2E
        lane = lax.broadcasted_iota(jnp.int32, (s_len, HEAD_DIM), 1)
        for h in heads:
            cols = slice(h * HEAD_DIM, (h + 1) * HEAD_DIM)
            hi, mid, lo = _split3(cum_keys[:, h:h + 1])
            pieces = jnp.where(lane == 0, hi.astype(F32),
                               jnp.where(lane == 1, mid.astype(F32),
                                         jnp.where(lane == 2, lo.astype(F32), 0.0)))
            kaug_ref[h, :, :HEAD_DIM] = k_ref[0, :, cols]
            kaug_ref[h, :, HEAD_DIM:] = pieces.astype(BF16)
            for t in range(s_len // tq):
                rows = slice(t * tq, (t + 1) * tq)
                vt_ref[h, :, rows] = v_ref[0, rows, cols].astype(F32).T.astype(BF16)

    lane = lax.broadcasted_iota(jnp.int32, (tq, HEAD_DIM), 1)
    minus_one = jnp.where(lane < 3, -1.0, 0.0).astype(BF16)
    q_aug = [jnp.concatenate(
        [(q_ref[0, :, h * HEAD_DIM:(h + 1) * HEAD_DIM].astype(F32)
          * (HEAD_DIM ** -0.5 * LOG2E)).astype(BF16), minus_one], axis=1) for h in heads]
    row = lax.broadcasted_iota(jnp.int32, (SUBLANES, LANES), 0)
    col = lax.broadcasted_iota(jnp.int32, (SUBLANES, LANES), 1)
    pick_rows = jnp.where(jnp.logical_and(row < hg, col == cum_col0 + head0 + row), 1.0, 0.0)
    p_hi, p_mid, p_lo = (_dot_nt(pick_rows.astype(BF16), piece) for piece in _split3(cumq_ref[0]))
    cum_q = ((p_hi + p_mid) + p_lo) * LOG2E
    cq = [cum_q[h:h + 1] for h in heads]

    def step(ki, carry, diagonal):
        ms, ls, accs = carry
        start = pl.multiple_of(ki * tq, tq)
        s_t = [_dot_nt(kaug_ref[h, pl.ds(start, tq), :], q_aug[h]) for h in heads]
        if diagonal:
            key = lax.broadcasted_iota(jnp.int32, (tq, tq), 0)
            qry = lax.broadcasted_iota(jnp.int32, (tq, tq), 1)
            s_t = [jnp.where(key <= qry, s, NEG_INF) for s in s_t]
        m_new = [jnp.maximum(ms[h], jnp.max(s_t[h], axis=0, keepdims=True) + cq[h]) for h in heads]
        p_t = [jnp.exp2(s_t[h] - (m_new[h] - cq[h])) for h in heads]
        alpha = [jnp.exp2(ms[h] - m_new[h]) for h in heads]
        l_new = [alpha[h] * ls[h] + jnp.sum(p_t[h], axis=0, keepdims=True) for h in heads]
        acc_new = [alpha[h] * accs[h] + _dot(vt_ref[h, :, pl.ds(start, tq)], p_t[h].astype(BF16))
                   for h in heads]
        return tuple(m_new), tuple(l_new), tuple(acc_new)

    init = (tuple(jnp.full((1, tq), NEG_INF, F32) for _ in heads),
            tuple(jnp.zeros((1, tq), F32) for _ in heads),
            tuple(jnp.zeros((HEAD_DIM, tq), F32) for _ in heads))
    carry = lax.fori_loop(0, qi, functools.partial(step, diagonal=False), init)
    _, ls, accs = step(qi, carry, diagonal=True)
    for h in heads:
        o_ref[0, :, h * HEAD_DIM:(h + 1) * HEAD_DIM] = (accs[h] / ls[h]).T.astype(o_ref.dtype)


FOX_HEADS_PER_STEP = 4


def _fox(proj_bf16, qkv_col0, cum, cum_col0, b, s, tq=512):
    tq = min(tq, s)
    hg = FOX_HEADS_PER_STEP
    width = N_HEADS * HEAD_DIM
    wb = hg * HEAD_DIM
    return pl.pallas_call(
        functools.partial(_fox_body, tq=tq, s_len=s, cum_col0=cum_col0),
        grid=(b, N_HEADS // hg, s // tq),
        in_specs=[pl.BlockSpec((1, tq, wb), lambda i, h, j: (i, j, qkv_col0 // wb + h)),
                  pl.BlockSpec((1, s, wb), lambda i, h, j: (i, 0, (qkv_col0 + width) // wb + h)),
                  pl.BlockSpec((1, s, wb), lambda i, h, j: (i, 0, (qkv_col0 + 2 * width) // wb + h)),
                  pl.BlockSpec((1, tq, LANES), lambda i, h, j: (i, j, 0)),
                  pl.BlockSpec((1, s, LANES), lambda i, h, j: (i, 0, 0))],
        out_specs=pl.BlockSpec((1, tq, wb), lambda i, h, j: (i, j, h)),
        out_shape=jax.ShapeDtypeStruct((b, s, width), BF16),
        scratch_shapes=[pltpu.VMEM((hg, s, FOX_AUG), BF16), pltpu.VMEM((hg, HEAD_DIM, s), BF16)],
        compiler_params=_params(("arbitrary", "arbitrary", "arbitrary"), 48),
        name="fox_attention",
    )(proj_bf16, proj_bf16, proj_bf16, cum, cum)


def _merge_body(yp_ref, yd_ref, yf_ref, wp_ref, wd_ref, wf_ref, g0_ref, g1_ref, g2_ref, o_ref):
    acc = g0_ref[...].astype(F32) * _dot(yp_ref[...], wp_ref[...])
    acc = acc + g1_ref[...].astype(F32) * _dot(yd_ref[...], wd_ref[...])
    acc = acc + g2_ref[...].astype(F32) * _dot(yf_ref[...], wf_ref[...])
    o_ref[...] = acc.astype(o_ref.dtype)


def _merge(y_pool, y_dn, y_fox, w_pool, w_dn, w_fox, gates, tm=512, tn=1024):
    m, k = y_pool.shape
    n = w_pool.shape[1]
    tm = min(tm, m)
    nb = n // tn
    y_spec = pl.BlockSpec((tm, k), lambda i, j: (i, 0))
    w_spec = pl.BlockSpec((k, tn), lambda i, j: (0, j))

    def gate_spec(branch):
        return pl.BlockSpec((tm, tn), lambda i, j: (i, branch * nb + j))

    return pl.pallas_call(
        _merge_body,
        grid=(m // tm, nb),
        in_specs=[y_spec, y_spec, y_spec, w_spec, w_spec, w_spec,
                  gate_spec(0), gate_spec(1), gate_spec(2)],
        out_specs=pl.BlockSpec((tm, tn), lambda i, j: (i, j)),
        out_shape=jax.ShapeDtypeStruct((m, n), BF16),
        compiler_params=_params(("arbitrary", "arbitrary"), 44),
        name="branch_merge",
    )(y_pool, y_dn, y_fox, w_pool, w_dn, w_fox, gates, gates, gates)


def _out_proj_body(a_ref, w_ref, x_ref, g_ref, nw_ref, sc_ref, sh_ref, x_out_ref, h_ref, ht_ref):
    x = x_ref[0] + g_ref[0] * _dot(a_ref[0], w_ref[...])
    x_out_ref[0] = x
    y = _rms(x, nw_ref[...]) * (1.0 + sc_ref[0]) + sh_ref[0]
    h_ref[0] = y.astype(h_ref.dtype)
    ht_ref[...] = y.T.astype(ht_ref.dtype)


def _out_proj(merged, w_out, x, gate, norm_w, sc, sh, tm=512):
    b, s, d = x.shape
    k = merged.shape[-1]
    tm = min(tm, s)
    ns = s // tm
    tile = pl.BlockSpec((1, tm, d), lambda i, j: (i, j, 0))
    per_batch = pl.BlockSpec((1, 1, d), lambda i, j: (i, 0, 0))
    return pl.pallas_call(
        _out_proj_body,
        grid=(b, ns),
        in_specs=[pl.BlockSpec((1, tm, k), lambda i, j: (i, j, 0)),
                  pl.BlockSpec((k, d), lambda i, j: (0, 0), pipeline_mode=pl.Buffered(1)),
                  tile, per_batch,
                  pl.BlockSpec((1, d), lambda i, j: (0, 0)), per_batch, per_batch],
        out_specs=[tile, tile, pl.BlockSpec((d, tm), lambda i, j: (0, i * ns + j))],
        out_shape=[jax.ShapeDtypeStruct((b, s, d), F32), jax.ShapeDtypeStruct((b, s, d), BF16),
                   jax.ShapeDtypeStruct((d, b * s), BF16)],
        compiler_params=_params(("arbitrary", "arbitrary"), 48),
        name="out_proj_residual",
    )(merged, w_out, x, gate, norm_w.reshape(1, d), sc, sh)


_CANDIDATES = [(a, b) for a in range(PEER_TOPK) for b in range(PEER_TOPK)
               if (a + 1) * (b + 1) <= PEER_TOPK]
_CROSS_PAIRS = [(c, d) for c in _CANDIDATES for d in _CANDIDATES if c[0] < d[0] and c[1] > d[1]]
_SELECT_INTERLEAVE = 4


def _peer_select_body(h_ref, wq_ref, sk_ref, e1_ref, cnt_ref, e2_ref, rank2_ref,
                      sc_ref, rk_ref, tp_ref, cn_ref, zi_ref, *, tt):
    n_lt = tt // LANES
    n_half = PEER_HEADS * n_lt
    q = _dot(h_ref[...], wq_ref[...])
    for hp in range(2 * PEER_HEADS):
        head, half = divmod(hp, 2)
        q_hp = q[:, hp * N_SUBKEYS:(hp + 1) * N_SUBKEYS].astype(BF16)
        s_t = _dot_nt(sk_ref[hp], q_hp)
        for lt in range(n_lt):
            sc_ref[half * n_half + head * n_lt + lt] = s_t[:, lt * LANES:(lt + 1) * LANES]

    key_id = lax.broadcasted_iota(jnp.int32, (N_SUBKEYS, LANES), 0).astype(F32)

    def half_topk(it, carry):
        base = it * _SELECT_INTERLEAVE
        vs = [sc_ref[base + k] for k in range(_SELECT_INTERLEAVE)]
        ranks = [jnp.full((N_SUBKEYS, LANES), float(PEER_TOPK), F32)] * _SELECT_INTERLEAVE
        for r in range(PEER_TOPK):
            ms = [jnp.max(v, axis=0, keepdims=True) for v in vs]
            firsts = [jnp.min(jnp.where(v == m, key_id, float(N_SUBKEYS)), axis=0, keepdims=True)
                      for v, m in zip(vs, ms)]
            hits = [key_id == first for first in firsts]
            ranks = [jnp.where(hit, float(r), rank) for hit, rank in zip(hits, ranks)]
            vs = [jnp.where(hit, NEG_INF, v) for hit, v in zip(hits, vs)]
            for k in range(_SELECT_INTERLEAVE):
                tp_ref[r, pl.ds(base + k, 1), :] = ms[k]
        for k in range(_SELECT_INTERLEAVE):
            rk_ref[base + k] = ranks[k]
        return carry

    lax.fori_loop(0, 2 * n_half // _SELECT_INTERLEAVE, half_topk, 0)

    def pair_topk(g, carry):
        b1 = pl.multiple_of(g * SUBLANES, SUBLANES)
        b2 = b1 + n_half
        t1 = [tp_ref[a, pl.ds(b1, SUBLANES), :] for a in range(PEER_TOPK)]
        t2 = [tp_ref[b, pl.ds(b2, SUBLANES), :] for b in range(PEER_TOPK)]
        cand = {(a, b): t1[a] + t2[b] for a, b in _CANDIDATES}
        beaten = {(a, b): jnp.full((SUBLANES, LANES), float((a + 1) * (b + 1) - 1), F32)
                  for a, b in _CANDIDATES}
        for c, d in _CROSS_PAIRS:
            c_first = cand[c] >= cand[d]
            beaten[d] = beaten[d] + jnp.where(c_first, 1.0, 0.0)
            beaten[c] = beaten[c] + jnp.where(c_first, 0.0, 1.0)
        z = jnp.zeros((SUBLANES, LANES), F32)
        cnts = [jnp.zeros((SUBLANES, LANES), F32) for _ in range(PEER_TOPK)]
        for a, b in _CANDIDATES:
            sel = jnp.where(beaten[(a, b)] < float(PEER_TOPK), 1.0, 0.0)
            cnts[a] = cnts[a] + sel
            z = z + sel * jnp.exp(cand[(a, b)] - cand[(0, 0)])
        for a in range(PEER_TOPK):
            cn_ref[a, pl.ds(b1, SUBLANES), :] = cnts[a]
        zi_ref[pl.ds(b1, SUBLANES), :] = 1.0 / z
        return carry

    lax.fori_loop(0, n_half // SUBLANES, pair_topk, 0)

    def emit(i1, carry):
        i2 = i1 + n_half
        head = i1 // n_lt
        lt = i1 % n_lt
        rank1 = rk_ref[i1]
        cnt = jnp.zeros((N_SUBKEYS, LANES), F32)
        for a in range(PEER_TOPK):
            cnt = jnp.where(rank1 == float(a), cn_ref[a, pl.ds(i1, 1), :], cnt)
        e1 = jnp.exp(sc_ref[i1] - tp_ref[0, pl.ds(i1, 1), :]) * zi_ref[pl.ds(i1, 1), :]
        e1_ref[head, lt] = e1
        cnt_ref[head, lt] = cnt
        e2 = jnp.exp(sc_ref[i2] - tp_ref[0, pl.ds(i2, 1), :])
        e2_ref[head, lt] = pltpu.bitcast(e2.astype(BF16), jnp.uint32)
        rank2_ref[head, lt] = pltpu.bitcast(rk_ref[i2].astype(BF16), jnp.uint32)
        return carry

    lax.fori_loop(0, n_half, emit, 0)


def _peer_select(h, w_q, sub_keys, tt=512):
    t, d = h.shape
    tt = min(tt, t)
    n_lt = tt // LANES
    n_prob = 2 * PEER_HEADS * n_lt
    dims = (PEER_HEADS, t // LANES, N_SUBKEYS, LANES)
    packed_dims = (PEER_HEADS, t // LANES, N_SUBKEYS // 2, LANES)
    spec = pl.BlockSpec((PEER_HEADS, n_lt, N_SUBKEYS, LANES), lambda i: (0, i, 0, 0))
    packed_spec = pl.BlockSpec((PEER_HEADS, n_lt, N_SUBKEYS // 2, LANES), lambda i: (0, i, 0, 0))
    return pl.pallas_call(
        functools.partial(_peer_select_body, tt=tt),
        grid=(t // tt,),
        in_specs=[pl.BlockSpec((tt, d), lambda i: (i, 0)),
                  pl.BlockSpec(w_q.shape, lambda i: (0, 0)),
                  pl.BlockSpec(sub_keys.shape, lambda i: (0, 0, 0))],
        out_specs=[spec, spec, packed_spec, packed_spec],
        out_shape=[jax.ShapeDtypeStruct(dims, F32), jax.ShapeDtypeStruct(dims, F32),
                   jax.ShapeDtypeStruct(packed_dims, jnp.uint32),
                   jax.ShapeDtypeStruct(packed_dims, jnp.uint32)],
        scratch_shapes=[pltpu.VMEM((n_prob, N_SUBKEYS, LANES), F32),
                        pltpu.VMEM((n_prob, N_SUBKEYS, LANES), F32),
                        pltpu.VMEM((PEER_TOPK, n_prob, LANES), F32),
                        pltpu.VMEM((PEER_TOPK, n_prob // 2, LANES), F32),
                        pltpu.VMEM((n_prob // 2, LANES), F32)],
        compiler_params=_params(("arbitrary",), 48),
        name="peer_select",
    )(h, w_q, sub_keys)


def _peer_dense_body(u0_ref, ht0_ref, u_ref, ht_ref, vt_ref, e1_ref, cnt_ref, e2_ref, rank2_ref,
                     o_ref, act_a_ref, act_b_ref, coef_ref, *, tt, te):
    n_lt = tt // LANES
    n_i = te // N_SUBKEYS
    bf16_rows = 2 * SUBLANES
    zero = jnp.zeros((), BF16)
    step = pl.program_id(0) * pl.num_programs(1) + pl.program_id(1)

    def row_bf16(ref, head, lt, ii):
        row = jnp.broadcast_to(ref[head, lt, ii:ii + 1, :], (bf16_rows, LANES)).astype(BF16)
        return jnp.tile(row, (N_SUBKEYS // bf16_rows, 1))

    @pl.when(step == 0)
    def _():
        act_a_ref[...] = _dot(u0_ref[...], ht0_ref[...])

    @pl.when(pl.program_id(1) == 0)
    def _():
        o_ref[...] = jnp.zeros_like(o_ref)

    def work(act_ref, next_act_ref):
        next_act_ref[...] = _dot(u_ref[...], ht_ref[...])
        for ii in range(n_i):
            rows = slice(ii * N_SUBKEYS, (ii + 1) * N_SUBKEYS)
            for lt in range(n_lt):
                lanes = slice(lt * LANES, (lt + 1) * LANES)
                w = None
                for head in range(PEER_HEADS):
                    cnt = row_bf16(cnt_ref, head, lt, ii)
                    e1 = row_bf16(e1_ref, head, lt, ii)
                    rank2 = pltpu.bitcast(rank2_ref[head, lt], BF16)
                    e2 = pltpu.bitcast(e2_ref[head, lt], BF16)
                    term = jnp.where(rank2 < cnt, e2, zero) * e1
                    w = term if w is None else w + term
                a = act_ref[rows, lanes]
                gelu = 0.5 * a * (1.0 + lax.erf(a * (2.0 ** -0.5)))
                coef_ref[rows, lanes] = gelu.astype(BF16) * w
        o_ref[...] += _dot(vt_ref[...], coef_ref[...])

    @pl.when(lax.rem(step, 2) == 0)
    def _():
        work(act_a_ref, act_b_ref)

    @pl.when(lax.rem(step, 2) == 1)
    def _():
        work(act_b_ref, act_a_ref)


def _peer_dense(u, h_t, v_t, e1, cnt, e2, rank2, tt=512, te=1024):
    n_exp, d = u.shape
    t = h_t.shape[1]
    tt = min(tt, t)
    n_lt = tt // LANES
    n_i = te // N_SUBKEYS
    ni, nj = t // tt, n_exp // te
    full = pl.BlockSpec((PEER_HEADS, n_lt, N_SUBKEYS // 2, LANES), lambda i, j: (0, i, 0, 0))
    rows = pl.BlockSpec((PEER_HEADS, n_lt, n_i, LANES), lambda i, j: (0, i, j, 0))
    once = pl.Buffered(1)

    def next_token_tile(i, j):
        return jnp.minimum(i + (j + 1) // nj, ni - 1)

    return pl.pallas_call(
        functools.partial(_peer_dense_body, tt=tt, te=te),
        grid=(ni, nj),
        in_specs=[pl.BlockSpec((te, d), lambda i, j: (0, 0), pipeline_mode=once),
                  pl.BlockSpec((d, tt), lambda i, j: (0, 0), pipeline_mode=once),
                  pl.BlockSpec((te, d), lambda i, j: ((j + 1) % nj, 0)),
                  pl.BlockSpec((d, tt), lambda i, j: (0, next_token_tile(i, j))),
                  pl.BlockSpec((d, te), lambda i, j: (0, j)),
                  rows, rows, full, full],
        out_specs=pl.BlockSpec((d, tt), lambda i, j: (0, i)),
        out_shape=jax.ShapeDtypeStruct((d, t), F32),
        scratch_shapes=[pltpu.VMEM((te, tt), F32), pltpu.VMEM((te, tt), F32),
                        pltpu.VMEM((te, tt), BF16)],
        compiler_params=_params(("arbitrary", "arbitrary"), 50),
        name="peer_dense",
    )(u, h_t, u, h_t, v_t, e1, cnt, e2, rank2)


def _peer_residual_body(x_ref, yt_ref, g_ref, w_ref, sc_ref, sh_ref, x_out_ref, h_ref):
    x = x_ref[0] + g_ref[0] * yt_ref[...].T
    x_out_ref[0] = x
    h_ref[0] = (_rms(x, w_ref[...]) * (1.0 + sc_ref[0]) + sh_ref[0]).astype(h_ref.dtype)


def _peer_residual_final_body(x_ref, yt_ref, g_ref, w_ref, o_ref):
    o_ref[0] = _rms(x_ref[0] + g_ref[0] * yt_ref[...].T, w_ref[...])


def _peer_residual(x, y_t, gate, norm_w, sc=None, sh=None, ts=512):
    b, s, d = x.shape
    ts = min(ts, s)
    ns = s // ts
    tile = pl.BlockSpec((1, ts, d), lambda i, j: (i, j, 0))
    per_batch = pl.BlockSpec((1, 1, d), lambda i, j: (i, 0, 0))
    in_specs = [tile, pl.BlockSpec((d, ts), lambda i, j: (0, i * ns + j)), per_batch,
                pl.BlockSpec((1, d), lambda i, j: (0, 0))]
    args = [x, y_t, gate, norm_w.reshape(1, d)]
    if sc is None:
        body, out_specs, out_shape = _peer_residual_final_body, tile, jax.ShapeDtypeStruct((b, s, d), F32)
    else:
        body = _peer_residual_body
        in_specs += [per_batch, per_batch]
        args += [sc, sh]
        out_specs = [tile, tile]
        out_shape = [jax.ShapeDtypeStruct((b, s, d), F32), jax.ShapeDtypeStruct((b, s, d), BF16)]
    return pl.pallas_call(
        body,
        grid=(b, ns),
        in_specs=in_specs,
        out_specs=out_specs,
        out_shape=out_shape,
        compiler_params=_params(("arbitrary", "arbitrary"), 44),
        name="peer_residual",
    )(*args)


def kernel(x, c, ada_w, ada_b, norm_mix_w, w_in, pool_w, pool_scale, dn_conv_w, dn_a_log,
           dn_dt_bias, dn_onorm_w, fox_f_bias, w_branch_pool, w_branch_dn, w_branch_fox,
           w_out, norm_ffn_w, peer_w_q, peer_sub_keys, peer_u, peer_v, final_norm_w):
    b, s, d = x.shape
    t = b * s
    depth = ada_w.shape[0]
    pool_width = pool_w.shape[1] * pool_w.shape[2]
    width = N_HEADS * HEAD_DIM
    nh = N_HEADS

    c8 = jnp.zeros((SUBLANES, d), F32).at[:b].set(c)
    mod = _ada(c8, ada_w, ada_b)[:, :b].reshape(depth, b, 6, 1, d)

    o_pool = 0
    o_dnqkv = o_pool + pool_width
    o_dnb = o_dnqkv + 3 * width
    o_dna = o_dnb + nh
    o_dng = o_dna + nh
    o_foxqkv = o_dng + width
    o_foxf = o_foxqkv + 3 * width
    o_gates = o_foxf + nh

    for l in range(depth):
        sh_m, sc_m, g_m, sh_f, sc_f, g_f = (mod[l, :, i] for i in range(6))
        wl = w_in[l]
        w_f32 = wl[:, o_pool:o_dnb].astype(BF16)
        w_b16 = jnp.concatenate([wl[:, o_dng:o_foxqkv], wl[:, o_foxqkv:o_foxf]], axis=1).astype(BF16)
        w_gate = wl[:, o_gates:].astype(BF16)
        w_small = jnp.concatenate(
            [wl[:, o_dnb:o_dng], wl[:, o_foxf:o_gates],
             jnp.zeros((d, LANES - 3 * nh), F32)], axis=1).astype(BF16)

        if l == 0:
            h = _norm_mod(x, norm_mix_w[l], sc_m, sh_m)
        h = h.reshape(t, d)
        proj_f32 = _matmul(h, w_f32, F32).reshape(b, s, -1)
        proj_b16 = _matmul(h, w_b16, BF16).reshape(b, s, -1)
        gates = _matmul(h, w_gate, BF16, sigmoid=True)
        small = _matmul(h, w_small, F32).reshape(b, s, LANES)

        y_pool = _pool(proj_f32, pool_w[l].astype(BF16), pool_scale[l], b, s)

        y_dn = _deltanet(proj_f32, pool_width, proj_b16, 0, small,
                         dn_conv_w[l], dn_a_log[l], dn_dt_bias[l], dn_onorm_w[l], b, s)

        bias_row = jnp.zeros((1, LANES), F32).at[0, 2 * nh:3 * nh].set(fox_f_bias[l])
        cum = _cumf(small, bias_row)
        y_fox = _fox(proj_b16, width, cum, 2 * nh, b, s)

        merged = _merge(y_pool.reshape(t, -1), y_dn.reshape(t, -1), y_fox.reshape(t, -1),
                        w_branch_pool[l].astype(BF16), w_branch_dn[l].astype(BF16),
                        w_branch_fox[l].astype(BF16), gates)
        x, h2, h2_t = _out_proj(merged.reshape(b, s, d), w_out[l].astype(BF16), x, g_m,
                                norm_ffn_w[l], sc_f, sh_f)
        sel = _peer_select(h2.reshape(t, d), peer_w_q[l].astype(BF16),
                           peer_sub_keys[l].reshape(2 * PEER_HEADS, N_SUBKEYS, -1).astype(BF16))
        y_t = _peer_dense(peer_u[l].astype(BF16), h2_t, peer_v[l].T.astype(BF16), *sel)
        if l + 1 < depth:
            x, h = _peer_residual(x, y_t, g_f, norm_mix_w[l + 1], mod[l + 1, :, 1], mod[l + 1, :, 0])
        else:
            out = _peer_residual(x, y_t, g_f, final_norm_w)
    return out
```
